```python
import math, functools
import jax, jax.numpy as jnp
from jax import lax
import numpy as np

D_MODEL = 1024
BATCH = 16
SEQ = 2048
DEPTH = 2
DEC_BATCH = 32
DEC_SEQ = 4
PAST_LEN = 16384
PAGE_SIZE = 128

A_HEADS = 8
A_HEAD_DIM = 64
IDX_HEADS = 8
IDX_DIM = 64
TOPK_MAX = 256
IDX_Q_BLOCK = 64
ROPE_THETA = 10000.0
B_HEADS = 4
B_KEY_DIM = 128
B_VAL_DIM = 128
GLR_CHUNK = 16
POOL_WINDOWS = (2, 4, 8, 16)
POOL_GROUP = D_MODEL // len(POOL_WINDOWS)
POOL_BUF = max(POOL_WINDOWS) - 1
D_FF = -(-8 * D_MODEL // (3 * 256)) * 256
PLE_DIM = 256
RMS_EPS = 1e-6
N_AB_LAYERS = (DEPTH + 1) // 2
N_C_LAYERS = DEPTH // 2
A_Q = A_HEADS * A_HEAD_DIM
IDX_Q = IDX_HEADS * IDX_DIM
B_K = B_HEADS * B_KEY_DIM
B_V = B_HEADS * B_VAL_DIM
IN_SIZES = (A_Q, A_Q, A_Q, IDX_Q, IDX_DIM, IDX_HEADS, B_K, B_K, B_V, B_V)
IN_COLS = sum(IN_SIZES)
IN_OFFSETS = tuple(sum(IN_SIZES[:n]) for n in range(1, len(IN_SIZES)))
MIX_WIDTH = A_Q + B_V

kernel_name = 'hybrid_dsa_hgrn2_pool_decode_step'


def rmsnorm(x, w):
    xf = x.astype(jnp.float32)
    y = xf * lax.rsqrt(jnp.mean(xf * xf, axis=-1, keepdims=True) + RMS_EPS)
    return (y * w.astype(jnp.float32)).astype(x.dtype)


def rope(x, pos):
    half = x.shape[-1] // 2
    inv = ROPE_THETA ** (-jnp.arange(half, dtype=jnp.float32) / half)
    ang = pos.astype(jnp.float32)[:, None] * inv[None, :]
    ang = ang.reshape(ang.shape[:1] + (1,) * (x.ndim - 3) + ang.shape[1:])
    c, s = jnp.cos(ang), jnp.sin(ang)
    xf = x.astype(jnp.float32)
    x1, x2 = xf[..., :half], xf[..., half:]
    return jnp.concatenate([x1 * c - x2 * s, x2 * c + x1 * s], axis=-1).astype(x.dtype)


def index_scores(qi, wi, ki):
    s = jnp.einsum('bthd,bsd->bths', qi.astype(jnp.float32), ki.astype(jnp.float32)) * (IDX_DIM ** -0.5)
    return jnp.einsum('bths,bth->bts', jax.nn.relu(s), wi.astype(jnp.float32) * (IDX_HEADS ** -0.5))


def sparse_attend(q, k_sel, v_sel, valid):
    lg = jnp.einsum('bthd,btkhd->bthk', q.astype(jnp.float32), k_sel.astype(jnp.float32)) * (A_HEAD_DIM ** -0.5)
    lg = jnp.where(valid[:, :, None, :], lg, -jnp.inf)
    p = jax.nn.softmax(lg, axis=-1)
    return jnp.einsum('bthk,btkhd->bthd', p, v_sel.astype(jnp.float32)).astype(q.dtype)


def dsa_prompt(q, k, v, qi, ki, wi):
    B, S = q.shape[:2]
    kk = min(TOPK_MAX, S // 4)
    qb = min(IDX_Q_BLOCK, S)
    nb = S // qb
    key_pos = jnp.arange(S)
    bidx = jnp.arange(B)[:, None, None]

    def blocks(a):
        return a.reshape((B, nb, qb) + a.shape[2:]).swapaxes(0, 1)

    def one_block(args):
        q_b, qi_b, wi_b, start = args
        qpos = start + jnp.arange(qb)
        sc = index_scores(qi_b, wi_b, ki)
        sc = jnp.where(key_pos[None, None, :] <= qpos[None, :, None], sc, -jnp.inf)
        _, idx = lax.top_k(sc, kk)
        valid = idx <= qpos[None, :, None]
        return sparse_attend(q_b, k[bidx, idx], v[bidx, idx], valid)

    out = lax.map(one_block, (blocks(q), blocks(qi), blocks(wi), jnp.arange(nb) * qb))
    return out.swapaxes(0, 1).reshape(B, S, A_HEADS, A_HEAD_DIM)


def dsa_sample(q, k, v, qi, ki, wi, *, layer, cache_k, cache_v, cache_kidx, page_table):
    Bd, T = q.shape[:2]
    past = page_table.shape[1] * PAGE_SIZE
    L = past + T
    kk = min(TOPK_MAX, L // 4)
    past_ki = cache_kidx[layer, page_table].reshape(Bd, past, IDX_DIM)
    all_ki = jnp.concatenate([past_ki, ki.astype(past_ki.dtype)], axis=1)
    qpos = past + jnp.arange(T)
    sc = index_scores(qi, wi, all_ki)
    sc = jnp.where(jnp.arange(L)[None, None, :] <= qpos[None, :, None], sc, -jnp.inf)
    _, idx = lax.top_k(sc, kk)
    valid = idx <= qpos[None, :, None]
    is_new = (idx >= past)[..., None, None]
    bidx = jnp.arange(Bd)[:, None, None]
    pidx = jnp.minimum(idx, past - 1)
    phys = page_table[bidx, pidx // PAGE_SIZE]
    off = pidx % PAGE_SIZE
    nidx = jnp.clip(idx - past, 0, T - 1)
    k_sel = jnp.where(is_new, k[bidx, nidx], cache_k[layer, phys, off].astype(k.dtype))
    v_sel = jnp.where(is_new, v[bidx, nidx], cache_v[layer, phys, off].astype(v.dtype))
    return sparse_attend(q, k_sel, v_sel, valid)


def glr_scan(q, k, v, log_f, s0):
    B, T, H, DK = q.shape
    C = math.gcd(T, GLR_CHUNK)
    n = T // C

    def chunks(a):
        return a.astype(jnp.float32).reshape((B, n, C) + a.shape[2:]).swapaxes(0, 1)

    causal = jnp.tril(jnp.ones((C, C), bool))[None, :, :, None, None]

    def step(S, inp):
        qc, kc, vc, lfc = inp
        b = jnp.cumsum(lfc, axis=1)
        decay = jnp.exp(jnp.where(causal, b[:, :, None] - b[:, None, :], -jnp.inf))
        att = jnp.einsum('bthd,bshd,btshd->bhts', qc, kc, decay)
        o = jnp.einsum('bhts,bshv->bthv', att, vc) + jnp.einsum('bthd,bhdv->bthv', qc * jnp.exp(b), S)
        b_last = b[:, -1]
        S = jnp.exp(b_last)[..., None] * S + jnp.einsum('bshd,bshv->bhdv', kc * jnp.exp(b_last[:, None] - b), vc)
        return S, o

    S, o = lax.scan(step, s0.astype(jnp.float32), (chunks(q), chunks(k), chunks(v), chunks(log_f)))
    return o.swapaxes(0, 1).reshape(B, T, H, v.shape[-1]), S


def glr_mix(bq, bf, bi, bg, lb, norm_w, s0):
    B, T, _ = bq.shape
    kshape = (B, T, B_HEADS, B_KEY_DIM)
    vshape = (B, T, B_HEADS, B_VAL_DIM)
    bf32 = bf.astype(jnp.float32)
    lb = lb.astype(jnp.float32)
    f = lb + (1.0 - lb) * jax.nn.sigmoid(bf32)
    k = (1.0 - lb) * jax.nn.sigmoid(-bf32)
    q = jax.nn.silu(bq)
    o, s = glr_scan(q.reshape(kshape), k.reshape(kshape), bi.reshape(vshape), jnp.log(f).reshape(kshape), s0)
    o = rmsnorm(o, norm_w) * jax.nn.silu(bg.astype(jnp.float32).reshape(vshape))
    return o.reshape(B, T, B_V).astype(bq.dtype), s.astype(s0.dtype)


def ab_mixer(h, pos, w_in, w_out, lb, glr_norm_w, s0, attend):
    B, T, _ = h.shape
    aq, ak, av, iq, ik, iw, bq, bf, bi, bg = jnp.split(h @ w_in, list(IN_OFFSETS), axis=-1)
    q = rope(aq.reshape(B, T, A_HEADS, A_HEAD_DIM), pos)
    k = rope(ak.reshape(B, T, A_HEADS, A_HEAD_DIM), pos)
    v = av.reshape(B, T, A_HEADS, A_HEAD_DIM)
    qi = rope(iq.reshape(B, T, IDX_HEADS, IDX_DIM), pos)
    ki = rope(ik, pos)
    att = attend(q, k, v, qi, ki, iw).reshape(B, T, A_Q)
    glr, s_new = glr_mix(bq, bf, bi, bg, lb, glr_norm_w, s0)
    y = jnp.concatenate([att, glr.astype(att.dtype)], axis=-1) @ w_out
    return y, k, v, ki, s_new


def pool_mix(h, prev, w_pool, scale):
    B, T, D = h.shape
    P = prev.shape[1]
    ext = jnp.concatenate([prev.astype(h.dtype), h], axis=1)
    cs = jnp.concatenate([jnp.zeros((B, 1, D), jnp.float32), jnp.cumsum(ext.astype(jnp.float32), axis=1)], axis=1)
    win = jnp.repeat(jnp.array(POOL_WINDOWS, jnp.int32), POOL_GROUP)
    end = P + 1 + jnp.arange(T)
    lo = jnp.maximum(end[:, None] - win[None, :], 0)
    cnt = jnp.minimum(end[:, None], win[None, :]).astype(jnp.float32)
    lo_sum = jnp.take_along_axis(cs, jnp.broadcast_to(lo[None], (B, T, D)), axis=1)
    mean = (cs[:, P + 1:] - lo_sum) / cnt
    z = (mean - h.astype(jnp.float32)).reshape(B, T, len(POOL_WINDOWS), POOL_GROUP)
    y = jnp.einsum('btgc,gcd->btgd', z, w_pool.astype(jnp.float32)).reshape(B, T, D) * scale.astype(jnp.float32)
    return y.astype(h.dtype), ext[:, -POOL_BUF:]


def swiglu(h, w_up, w_down):
    gate, up = jnp.split(h @ w_up, 2, axis=-1)
    return (jax.nn.silu(gate) * up) @ w_down


def trunk(x, p, pos, attends, glr_s0, pool_prev, W):
    new_k, new_v, new_ki, new_s, new_buf = [], [], [], [], []
    lb_all = jnp.cumsum(jax.nn.softmax(W['lb_logits'].astype(jnp.float32), axis=0), axis=0)
    for i in range(DEPTH):
        j = i // 2
        h = rmsnorm(x, W['mix_norm'][i])
        if i % 2 == 0:
            y, k, v, ki, s = ab_mixer(h, pos, W['w_in'][j], W['w_out'][j], lb_all[j], W['glr_norm'][j], glr_s0[j], attends[j])
            new_k.append(k)
            new_v.append(v)
            new_ki.append(ki)
            new_s.append(s)
        else:
            y, buf = pool_mix(h, pool_prev[j], W['pool_w'][j], W['pool_scale'][j])
            new_buf.append(buf)
        x = x + y
        x = x + swiglu(rmsnorm(x, W['ffn_norm'][i]), W['w_up'][i], W['w_down'][i])
        gate = jax.nn.sigmoid(rmsnorm(x, W['ple_norm'][i]) @ W['w_ple_gate'][i])
        x = x + (p[i] @ W['w_ple'][i]) * gate
    return (rmsnorm(x, W['final_norm']), jnp.stack(new_k), jnp.stack(new_v), jnp.stack(new_ki),
            jnp.stack(new_s), jnp.stack(new_buf))


def setup_inputs(seed: int = 0) -> dict:
    key = jax.random.key(seed)
    ks = jax.random.split(key, 32)
    f32 = jnp.float32
    n_pages = PAST_LEN // PAGE_SIZE
    n_used = DEC_BATCH * n_pages
    n_pool = n_used + n_used // 4

    def nrm(k, shape, s=1.0):
        return jax.random.normal(k, shape, f32) * s

    def gain(k, shape):
        return 1.0 + 0.02 * jax.random.normal(k, shape, f32)

    page_table = jax.random.permutation(ks[7], n_pool)[:n_used].reshape(DEC_BATCH, n_pages).astype(jnp.int32)
    return {
        'x_prompt': nrm(ks[0], (BATCH, SEQ, D_MODEL)),
        'x_sample': nrm(ks[1], (DEC_BATCH, DEC_SEQ, D_MODEL)),
        'cache_k': nrm(ks[2], (N_AB_LAYERS, n_pool, PAGE_SIZE, A_HEADS, A_HEAD_DIM)),
        'cache_v': nrm(ks[3], (N_AB_LAYERS, n_pool, PAGE_SIZE, A_HEADS, A_HEAD_DIM)),
        'cache_kidx': nrm(ks[4], (N_AB_LAYERS, n_pool, PAGE_SIZE, IDX_DIM)),
        'state_glr': nrm(ks[5], (N_AB_LAYERS, DEC_BATCH, B_HEADS, B_KEY_DIM, B_VAL_DIM), 0.5),
        'state_pool': nrm(ks[6], (N_C_LAYERS, DEC_BATCH, POOL_BUF, D_MODEL)),
        'page_table': page_table,
        'p_prompt': nrm(ks[8], (DEPTH, BATCH, SEQ, PLE_DIM)),
        'p_sample': nrm(ks[9], (DEPTH, DEC_BATCH, DEC_SEQ, PLE_DIM)),
        'mix_norm': gain(ks[10], (DEPTH, D_MODEL)),
        'w_in': nrm(ks[11], (N_AB_LAYERS, D_MODEL, IN_COLS), D_MODEL ** -0.5),
        'w_out': nrm(ks[12], (N_AB_LAYERS, MIX_WIDTH, D_MODEL), MIX_WIDTH ** -0.5),
        'lb_logits': nrm(ks[13], (N_AB_LAYERS + 1, B_K), 0.5),
        'glr_norm': gain(ks[14], (N_AB_LAYERS, B_VAL_DIM)),
        'pool_w': nrm(ks[15], (N_C_LAYERS, len(POOL_WINDOWS), POOL_GROUP, POOL_GROUP), POOL_GROUP ** -0.5),
        'pool_scale': gain(ks[16], (N_C_LAYERS, D_MODEL)),
        'ffn_norm': gain(ks[17], (DEPTH, D_MODEL)),
        'w_up': nrm(ks[18], (DEPTH, D_MODEL, 2 * D_FF), D_MODEL ** -0.5),
        'w_down': nrm(ks[19], (DEPTH, D_FF, D_MODEL), D_FF ** -0.5),
        'ple_norm': gain(ks[20], (DEPTH, D_MODEL)),
        'w_ple_gate': nrm(ks[21], (DEPTH, D_MODEL, D_MODEL), D_MODEL ** -0.5),
        'w_ple': nrm(ks[22], (DEPTH, PLE_DIM, D_MODEL), PLE_DIM ** -0.5),
        'final_norm': gain(ks[23], (D_MODEL,)),
    }


def reference(x_prompt, x_sample, cache_k, cache_v, cache_kidx, state_glr, state_pool, page_table,
              p_prompt, p_sample, mix_norm, w_in, w_out, lb_logits, glr_norm, pool_w, pool_scale,
              ffn_norm, w_up, w_down, ple_norm, w_ple_gate, w_ple, final_norm):
    W = dict(mix_norm=mix_norm, w_in=w_in, w_out=w_out, lb_logits=lb_logits, glr_norm=glr_norm,
             pool_w=pool_w, pool_scale=pool_scale, ffn_norm=ffn_norm, w_up=w_up, w_down=w_down,
             ple_norm=ple_norm, w_ple_gate=w_ple_gate, w_ple=w_ple, final_norm=final_norm)
    Bp, Sp = x_prompt.shape[:2]
    pos_p = jnp.arange(Sp, dtype=jnp.int32)
    glr0_p = [jnp.zeros((Bp, B_HEADS, B_KEY_DIM, B_VAL_DIM), state_glr.dtype) for _ in range(N_AB_LAYERS)]
    pool0_p = [jnp.zeros((Bp, 0, D_MODEL), x_prompt.dtype) for _ in range(N_C_LAYERS)]
    yp, kp, vp, kip, gp, pp = trunk(x_prompt, p_prompt, pos_p, [dsa_prompt] * N_AB_LAYERS, glr0_p, pool0_p, W)
    past = page_table.shape[1] * PAGE_SIZE
    pos_s = past + jnp.arange(x_sample.shape[1], dtype=jnp.int32)
    attends_s = [functools.partial(dsa_sample, layer=j, cache_k=cache_k, cache_v=cache_v,
                                   cache_kidx=cache_kidx, page_table=page_table) for j in range(N_AB_LAYERS)]
    glr0_s = [state_glr[j] for j in range(N_AB_LAYERS)]
    pool0_s = [state_pool[j] for j in range(N_C_LAYERS)]
    ys, ks_, vs_, kis, gs, ps = trunk(x_sample, p_sample, pos_s, attends_s, glr0_s, pool0_s, W)
    return (yp, ys, kp, vp, kip, gp, pp, ks_, vs_, kis, gs, ps)
```

```python
import functools

import jax
import jax.numpy as jnp
from jax import lax
from jax.experimental import pallas as pl
from jax.experimental.pallas import tpu as pltpu

F32 = jnp.float32
BF16 = jnp.bfloat16

A_HEADS = 8
A_HEAD_DIM = 64
IDX_HEADS = 8
IDX_DIM = 64
TOPK_MAX = 256
ROPE_THETA = 10000.0
B_HEADS = 4
B_KEY_DIM = 128
B_VAL_DIM = 128
GLR_BLOCK = 16
POOL_WINDOWS = (2, 4, 8, 16)
POOL_HALO = 16
PAGE_SIZE = 128
RMS_EPS = 1e-6

A_Q = A_HEADS * A_HEAD_DIM
IDX_Q = IDX_HEADS * IDX_DIM
B_K = B_HEADS * B_KEY_DIM
B_V = B_HEADS * B_VAL_DIM

LANES = 128
VMEM_LIMIT = 56 * 1024 * 1024
INT_MIN = -(2 ** 31)
NEG_INF = float("-inf")


def _params(*sem):
    return pltpu.CompilerParams(dimension_semantics=sem, vmem_limit_bytes=VMEM_LIMIT)


def _rms(x, g):
    ms = jnp.mean(x * x, axis=-1, keepdims=True)
    return x * lax.rsqrt(ms + RMS_EPS) * g


def _dot(a, b):
    return jnp.dot(a, b, preferred_element_type=F32)


def _dot_nt(a, b):
    return lax.dot_general(a, b, (((1,), (1,)), ((), ())), preferred_element_type=F32)


def _const_spec(shape):
    nd = len(shape)
    return pl.BlockSpec(shape, lambda *_: (0,) * nd, pipeline_mode=pl.Buffered(1))


IN_GROUP_A = 4 * A_Q
IN_GROUP_I = LANES
IN_GROUP_B = 2 * B_K + 2 * B_V
IN_PACKED = IN_GROUP_A + IN_GROUP_I + IN_GROUP_B


def _in_proj_kernel(x_ref, g_ref, w_ref, cos_ref, sin_ref,
                    qb_ref, kf_ref, kb_ref, vf_ref, vb_ref, qib_ref, kiw_ref, ki2_ref, bg_ref):
    tm = x_ref.shape[0]
    h = _rms(x_ref[...], g_ref[...]).astype(BF16)
    cos = cos_ref[...]
    sin = sin_ref[...]
    lane = lax.broadcasted_iota(jnp.int32, (tm, LANES), 1)
    first = (lane % A_HEAD_DIM) < (A_HEAD_DIM // 2)

    def rope(z, c, s):
        rot = jnp.where(first, pltpu.roll(z, LANES - A_HEAD_DIM // 2, 1), pltpu.roll(z, A_HEAD_DIM // 2, 1))
        return z * c + rot * s

    def roped(col0):
        z = _dot(h, w_ref[:, col0:col0 + A_Q])
        return [rope(z[:, LANES * i:LANES * (i + 1)], cos, sin) for i in range(A_Q // LANES)]

    scale = A_HEAD_DIM ** -0.5
    for i, r in enumerate(roped(0)):
        qb_ref[:, LANES * i:LANES * (i + 1)] = (r * scale).astype(BF16)
    for i, r in enumerate(roped(A_Q)):
        kf_ref[:, LANES * i:LANES * (i + 1)] = r
        kb_ref[:, LANES * i:LANES * (i + 1)] = r.astype(BF16)
    v = _dot(h, w_ref[:, 2 * A_Q:3 * A_Q])
    vf_ref[...] = v
    vb_ref[...] = v.astype(BF16)
    for i, r in enumerate(roped(3 * A_Q)):
        qib_ref[:, LANES * i:LANES * (i + 1)] = (r * (IDX_DIM ** -0.5)).astype(BF16)
    is_key = lane < IDX_DIM
    z = _dot(h, w_ref[:, IN_GROUP_A:IN_GROUP_A + IN_GROUP_I])
    kiw = rope(z, jnp.where(is_key, cos, 1.0), jnp.where(is_key, sin, 0.0))
    kiw_ref[...] = kiw
    ki2_ref[...] = jnp.where(is_key, kiw, pltpu.roll(kiw, IDX_DIM, 1)).astype(BF16)
    col0 = IN_GROUP_A + IN_GROUP_I
    for i in range(IN_GROUP_B // A_Q):
        bg_ref[:, A_Q * i:A_Q * (i + 1)] = _dot(h, w_ref[:, col0 + A_Q * i:col0 + A_Q * (i + 1)])


def _in_proj(x, g, w_packed, cos_tab, sin_tab, tm):
    n, d = x.shape
    n_tab = cos_tab.shape[0] // tm
    row = lambda width: pl.BlockSpec((tm, width), lambda i: (i, 0))
    out_shape = (
        jax.ShapeDtypeStruct((n, A_Q), BF16),
        jax.ShapeDtypeStruct((n, A_Q), F32),
        jax.ShapeDtypeStruct((n, A_Q), BF16),
        jax.ShapeDtypeStruct((n, A_Q), F32),
        jax.ShapeDtypeStruct((n, A_Q), BF16),
        jax.ShapeDtypeStruct((n, IDX_Q), BF16),
        jax.ShapeDtypeStruct((n, LANES), F32),
        jax.ShapeDtypeStruct((n, LANES), BF16),
        jax.ShapeDtypeStruct((n, IN_GROUP_B), F32),
    )
    return pl.pallas_call(
        _in_proj_kernel,
        grid=(n // tm,),
        in_specs=[row(d), _const_spec((1, d)), _const_spec((d, IN_PACKED)),
                  pl.BlockSpec((tm, LANES), lambda i: (i % n_tab, 0)),
                  pl.BlockSpec((tm, LANES), lambda i: (i % n_tab, 0))],
        out_specs=tuple(row(s.shape[1]) for s in out_shape),
        out_shape=out_shape,
        compiler_params=_params("parallel"),
    )(x, g, w_packed, cos_tab, sin_tab)


def _order_key(x):
    bits = pltpu.bitcast(x + 0.0, jnp.int32)
    return bits ^ ((bits >> 31) & jnp.int32(0x7FFFFFFF))


def _kth_largest_key(key_ref, kk, rows):
    def body(i, t):
        cand = t | jnp.left_shift(jnp.int32(1), 31 - i)
        ge = key_ref[...] >= (cand ^ jnp.int32(INT_MIN))
        cnt = jnp.sum(jnp.where(ge, 1.0, 0.0), axis=1, keepdims=True)
        return jnp.where(cnt >= kk, cand, t)

    t = lax.fori_loop(0, 32, body, jnp.zeros((rows, 1), jnp.int32))
    return t ^ jnp.int32(INT_MIN)


def _select_bias(key_ref, bias_ref, kk, use_topk, admissible):
    rows, width = key_ref.shape
    thr = _kth_largest_key(key_ref, float(kk), rows)
    key = key_ref[...]
    ge = key >= thr
    cnt_ge = jnp.sum(jnp.where(ge, 1.0, 0.0), axis=1, keepdims=True)
    bias_ref[...] = jnp.where(use_topk, jnp.where(ge, 0.0, NEG_INF), jnp.where(admissible, 0.0, NEG_INF))
    tied = jnp.logical_and(use_topk, cnt_ge != float(kk))

    @pl.when(jnp.max(jnp.where(tied, 1.0, 0.0)) > 0.0)
    def _():
        cnt_gt = jnp.sum(jnp.where(key > thr, 1.0, 0.0), axis=1, keepdims=True)
        room = float(kk) - cnt_gt
        before = (lax.broadcasted_iota(jnp.int32, (LANES, LANES), 0)
                  < lax.broadcasted_iota(jnp.int32, (LANES, LANES), 1))
        tri = jnp.where(before, 1.0, 0.0).astype(BF16)
        seen = jnp.zeros((rows, 1), F32)
        for c in range(width // LANES):
            sl = slice(LANES * c, LANES * (c + 1))
            kc = key_ref[:, sl]
            eq = jnp.where(kc == thr, 1.0, 0.0)
            rank = _dot(eq.astype(BF16), tri) + seen
            keep = jnp.logical_or(kc > thr, jnp.logical_and(kc == thr, rank < room))
            bias_ref[:, sl] = jnp.where(use_topk, jnp.where(keep, 0.0, NEG_INF), bias_ref[:, sl])
            seen = seen + jnp.sum(eq, axis=1, keepdims=True)


def _pair_masks(rows):
    lane = lax.broadcasted_iota(jnp.int32, (rows, LANES), 1)
    lo = lane < A_HEAD_DIM
    return lo, jnp.logical_not(lo)


def _index_scores(qi_ref, ki2, w):
    rows = qi_ref.shape[0]
    masks = _pair_masks(rows)
    acc = None
    for p in range(IDX_Q // LANES):
        slab = qi_ref[:, LANES * p:LANES * (p + 1)]
        for half in range(2):
            s = _dot_nt(jnp.where(masks[half], slab, jnp.zeros_like(slab)), ki2)
            hh = 2 * p + half
            term = jnp.maximum(s, 0.0) * w[:, hh:hh + 1]
            acc = term if acc is None else acc + term
    return acc


def _masked_attention(q_ref, k_ref, v_ref, bias, o_ref):
    rows = q_ref.shape[0]
    masks = _pair_masks(rows)
    for p in range(A_Q // LANES):
        sl = slice(LANES * p, LANES * (p + 1))
        slab = q_ref[:, sl]
        ks = k_ref[:, sl]
        vs = v_ref[:, sl]
        outs = []
        for half in range(2):
            s = _dot_nt(jnp.where(masks[half], slab, jnp.zeros_like(slab)), ks) + bias
            m = jnp.max(s, axis=1, keepdims=True)
            e = jnp.exp(s - m)
            l = jnp.sum(e, axis=1, keepdims=True)
            outs.append(_dot(e.astype(BF16), vs) / l)
        o_ref[:, sl] = jnp.where(masks[0], outs[0], outs[1]).astype(o_ref.dtype)


def _dsa_prompt_kernel(kk, q_ref, qi_ref, kiw_ref, ki2_ref, k_ref, v_ref, o_ref, key_ref, bias_ref):
    tq = q_ref.shape[0]
    s_len = k_ref.shape[0]
    j = pl.program_id(1)
    w = kiw_ref[:, IDX_DIM:IDX_DIM + IDX_HEADS] * (IDX_HEADS ** -0.5)
    scores = _index_scores(qi_ref, ki2_ref[...], w)
    qpos = j * tq + lax.broadcasted_iota(jnp.int32, (tq, 1), 0)
    kpos = lax.broadcasted_iota(jnp.int32, (1, s_len), 1)
    causal = kpos <= qpos
    key_ref[...] = _order_key(jnp.where(causal, scores, NEG_INF))
    _select_bias(key_ref, bias_ref, kk, qpos >= kk, causal)
    _masked_attention(q_ref, k_ref, v_ref, bias_ref[...], o_ref)


def _dsa_prompt(qb, qib, kiw, ki2, kb, vb, tq):
    b, s, _ = qb.shape
    kk = min(TOPK_MAX, s // 4)
    tile = lambda width: pl.BlockSpec((None, tq, width), lambda bi, j: (bi, j, 0))
    full = lambda width: pl.BlockSpec((None, s, width), lambda bi, j: (bi, 0, 0))
    return pl.pallas_call(
        functools.partial(_dsa_prompt_kernel, kk),
        grid=(b, s // tq),
        in_specs=[tile(A_Q), tile(IDX_Q), tile(LANES), full(LANES), full(A_Q), full(A_Q)],
        out_specs=tile(A_Q),
        out_shape=jax.ShapeDtypeStruct((b, s, A_Q), BF16),
        scratch_shapes=[pltpu.VMEM((tq, s), jnp.int32), pltpu.VMEM((tq, s), F32)],
        compiler_params=_params("parallel", "arbitrary"),
    )(qb, qib, kiw, ki2, kb, vb)


TPAD = 8
NEW_PAD = LANES
CHUNK_PAGES = 16


def _dsa_sample_kernel(t_new, n_pages, kk, pt_ref, qbd_ref, qir_ref, wr_ref, knew_ref, vnew_ref, kinew_ref,
                       ckidx_ref, ck_ref, cv_ref, o_ref,
                       kibuf, kvbuf, logit_ref, key_ref, bias_ref, sem_ki, sem_kv):
    b = pl.program_id(0)
    past = n_pages * PAGE_SIZE
    width = past + NEW_PAD
    rows = A_HEADS * TPAD
    n_chunks = n_pages // CHUNK_PAGES

    def ki_copy(p):
        return pltpu.make_async_copy(ckidx_ref.at[pt_ref[b, p]], kibuf.at[pl.ds(p * PAGE_SIZE, PAGE_SIZE)], sem_ki)

    def kv_copy(src_ref, c, i, slot):
        page = pt_ref[b, c * CHUNK_PAGES + i]
        return pltpu.make_async_copy(src_ref.at[page], kvbuf.at[slot, pl.ds(i * PAGE_SIZE, PAGE_SIZE)],
                                     sem_kv.at[slot])

    def start_chunk(src_ref, c, slot):
        lax.fori_loop(0, CHUNK_PAGES, lambda i, _: (kv_copy(src_ref, c, i, slot).start(), 0)[1], 0)

    def wait_chunk(src_ref, c, slot):
        lax.fori_loop(0, CHUNK_PAGES, lambda i, _: (kv_copy(src_ref, c, i, slot).wait(), 0)[1], 0)

    lax.fori_loop(0, n_pages, lambda p, _: (ki_copy(p).start(), 0)[1], 0)
    start_chunk(ck_ref, 0, 0)
    lax.fori_loop(0, n_pages, lambda p, _: (ki_copy(p).wait(), 0)[1], 0)

    qir = qir_ref[...]
    w = wr_ref[...] * (IDX_HEADS ** -0.5)

    def head_sum(s):
        t = jnp.maximum(s, 0.0) * w
        return jnp.sum(t.reshape(IDX_HEADS, TPAD, s.shape[1]), axis=0)

    tok = lax.broadcasted_iota(jnp.int32, (TPAD, 1), 0)
    tok_ok = tok < t_new
    for c in range(n_chunks):
        sl = slice(c * CHUNK_PAGES * PAGE_SIZE, (c + 1) * CHUNK_PAGES * PAGE_SIZE)
        sc = head_sum(_dot_nt(qir, kibuf[sl, :].astype(BF16)))
        key_ref[:, sl] = _order_key(jnp.where(tok_ok, sc, NEG_INF))
    sc_new = head_sum(_dot_nt(qir, kinew_ref[...]))
    new_idx = lax.broadcasted_iota(jnp.int32, (1, NEW_PAD), 1)
    new_ok = jnp.logical_and(jnp.logical_and(new_idx <= tok, new_idx < t_new), tok_ok)
    key_ref[:, past:] = _order_key(jnp.where(new_ok, sc_new, NEG_INF))
    admissible = jnp.logical_and(lax.broadcasted_iota(jnp.int32, (1, width), 1) <= past + tok, tok_ok)
    admissible = jnp.logical_and(admissible, lax.broadcasted_iota(jnp.int32, (1, width), 1) < past + t_new)
    use_topk = jnp.logical_and(past + tok + 1 > kk, tok_ok)
    _select_bias(key_ref, bias_ref, kk, use_topk, admissible)

    qbd = qbd_ref[...]
    for c in range(n_chunks):
        slot = c % 2
        wait_chunk(ck_ref, c, slot)
        if c + 1 < n_chunks:
            start_chunk(ck_ref, c + 1, 1 - slot)
        else:
            start_chunk(cv_ref, 0, 1 - slot)
        sl = slice(c * CHUNK_PAGES * PAGE_SIZE, (c + 1) * CHUNK_PAGES * PAGE_SIZE)
        logit_ref[:, sl] = _dot_nt(qbd, kvbuf[slot].astype(BF16))
    logit_ref[:, past:] = _dot_nt(qbd, knew_ref[...])
    bias = bias_ref[...]
    s = (logit_ref[...].reshape(A_HEADS, TPAD, width) + bias[None]).reshape(rows, width)
    s = jnp.where(jnp.concatenate([tok_ok] * A_HEADS, axis=0), s, 0.0)
    m = jnp.max(s, axis=1, keepdims=True)
    e = jnp.exp(s - m)
    l = jnp.sum(e, axis=1, keepdims=True)
    logit_ref[...] = e

    acc = _dot(logit_ref[:, past:].astype(BF16), vnew_ref[...])
    for c in range(n_chunks):
        slot = (n_chunks + c) % 2
        wait_chunk(cv_ref, c, slot)
        if c + 1 < n_chunks:
            start_chunk(cv_ref, c + 1, 1 - slot)
        sl = slice(c * CHUNK_PAGES * PAGE_SIZE, (c + 1) * CHUNK_PAGES * PAGE_SIZE)
        acc = acc + _dot(logit_ref[:, sl].astype(BF16), kvbuf[slot].astype(BF16))
    acc = (acc / l).reshape(A_HEADS, TPAD, A_Q)
    head_of_lane = lax.broadcasted_iota(jnp.int32, (1, TPAD, A_Q), 2) // A_HEAD_DIM
    out = jnp.zeros((TPAD, A_Q), F32)
    for hh in range(A_HEADS):
        out = out + jnp.where(head_of_lane[0] == hh, acc[hh], 0.0)
    o_ref[...] = out.astype(o_ref.dtype)


def _dsa_sample(qb, qib, kiw, ki2, kb, vb, cache_k, cache_v, cache_kidx, page_table, t_new):
    bd, n_pages = page_table.shape
    assert n_pages % CHUNK_PAGES == 0 and t_new <= TPAD
    past = n_pages * PAGE_SIZE
    kk = min(TOPK_MAX, (past + t_new) // 4)
    width = past + NEW_PAD
    rows = A_HEADS * TPAD

    def pad_tokens(a, to):
        a = a.reshape(bd, t_new, a.shape[-1])
        return jnp.pad(a, ((0, 0), (0, to - t_new), (0, 0)))

    q8 = pad_tokens(qb, TPAD)
    head_of_lane = jnp.arange(A_Q) // A_HEAD_DIM
    qbd = jnp.where(head_of_lane[None, None, None, :] == jnp.arange(A_HEADS)[None, :, None, None],
                    q8[:, None], jnp.zeros((), BF16)).reshape(bd, rows, A_Q)
    qir = pad_tokens(qib, TPAD).reshape(bd, TPAD, IDX_HEADS, IDX_DIM).transpose(0, 2, 1, 3).reshape(bd, rows, IDX_DIM)
    wr = pad_tokens(kiw[:, IDX_DIM:IDX_DIM + IDX_HEADS], TPAD).transpose(0, 2, 1).reshape(bd, rows, 1)
    knew = pad_tokens(kb, NEW_PAD)
    vnew = pad_tokens(vb, NEW_PAD)
    kinew = pad_tokens(ki2[:, :IDX_DIM], NEW_PAD)

    per_seq = lambda r, c: pl.BlockSpec((None, r, c), lambda b, pt: (b, 0, 0))
    any_spec = pl.BlockSpec(memory_space=pl.ANY)
    grid_spec = pltpu.PrefetchScalarGridSpec(
        num_scalar_prefetch=1,
        grid=(bd,),
        in_specs=[per_seq(rows, A_Q), per_seq(rows, IDX_DIM), per_seq(rows, 1), per_seq(NEW_PAD, A_Q),
                  per_seq(NEW_PAD, A_Q), per_seq(NEW_PAD, IDX_DIM), any_spec, any_spec, any_spec],
        out_specs=per_seq(TPAD, A_Q),
        scratch_shapes=[
            pltpu.VMEM((past, IDX_DIM), F32),
            pltpu.VMEM((2, CHUNK_PAGES * PAGE_SIZE, A_Q), F32),
            pltpu.VMEM((rows, width), F32),
            pltpu.VMEM((TPAD, width), jnp.int32),
            pltpu.VMEM((TPAD, width), F32),
            pltpu.SemaphoreType.DMA(()),
            pltpu.SemaphoreType.DMA((2,)),
        ],
    )
    out = pl.pallas_call(
        functools.partial(_dsa_sample_kernel, t_new, n_pages, kk),
        grid_spec=grid_spec,
        out_shape=jax.ShapeDtypeStruct((bd, TPAD, A_Q), BF16),
        compiler_params=_params("arbitrary"),
    )(page_table, qbd, qir, wr, knew, vnew, kinew,
      cache_kidx, cache_k.reshape(cache_k.shape[0], PAGE_SIZE, A_Q), cache_v.reshape(cache_v.shape[0], PAGE_SIZE, A_Q))
    return out[:, :t_new].reshape(bd * t_new, A_Q)


def _glr_kernel(t_valid, layer, bg_ref, lbl_ref, gw_ref, s0_ref, o_ref, s_ref, st_ref):
    tt = bg_ref.shape[0]
    nb = tt // GLR_BLOCK
    it = pl.program_id(1)

    @pl.when(it == 0)
    def _():
        for hh in range(B_HEADS):
            st_ref[hh] = s0_ref[hh].astype(F32).T

    logits = lbl_ref[...]
    ex = jnp.exp(logits - jnp.max(logits, axis=0, keepdims=True))
    lb_all = jnp.sum(ex[:layer + 1], axis=0, keepdims=True) / jnp.sum(ex, axis=0, keepdims=True)

    row = lax.broadcasted_iota(jnp.int32, (tt, 1), 0)
    valid = (it * tt + row) < t_valid
    rloc = row % GLR_BLOCK
    t_idx = lax.broadcasted_iota(jnp.int32, (1, GLR_BLOCK, 1), 1)
    blk_of_col = lax.broadcasted_iota(jnp.int32, (nb, 1, tt), 2) // GLR_BLOCK
    blk_id = lax.broadcasted_iota(jnp.int32, (nb, 1, tt), 0)
    in_block = blk_of_col == blk_id
    gw = gw_ref[...]

    for hh in range(B_HEADS):
        lb = lb_all[:, B_KEY_DIM * hh:B_KEY_DIM * (hh + 1)]
        bq = bg_ref[:, B_KEY_DIM * hh:B_KEY_DIM * (hh + 1)]
        bf = bg_ref[:, B_K + B_KEY_DIM * hh:B_K + B_KEY_DIM * (hh + 1)]
        bi = bg_ref[:, 2 * B_K + B_VAL_DIM * hh:2 * B_K + B_VAL_DIM * (hh + 1)]
        bo = bg_ref[:, 2 * B_K + B_V + B_VAL_DIM * hh:2 * B_K + B_V + B_VAL_DIM * (hh + 1)]
        f = lb + (1.0 - lb) * jax.nn.sigmoid(bf)
        kx = jnp.where(valid, (1.0 - lb) * jax.nn.sigmoid(-bf), 0.0)
        qx = bq * jax.nn.sigmoid(bq)
        cum = jnp.where(valid, jnp.log(f), 0.0)
        for sh in (1, 2, 4, 8):
            cum = cum + jnp.where(rloc >= sh, pltpu.roll(cum, sh, 0), 0.0)
        b3 = cum.reshape(nb, GLR_BLOCK, B_KEY_DIM)
        q3 = qx.reshape(nb, GLR_BLOCK, B_KEY_DIM)
        k3 = kx.reshape(nb, GLR_BLOCK, B_KEY_DIM)
        v3 = bi.reshape(nb, GLR_BLOCK, B_VAL_DIM)
        b_end = b3[:, GLR_BLOCK - 1:GLR_BLOCK, :]
        q_dec = (q3 * jnp.exp(b3)).astype(BF16)
        k_dec = (k3 * jnp.exp(b_end - b3)).reshape(tt, B_KEY_DIM).astype(BF16)

        o_in = jnp.zeros((nb, GLR_BLOCK, B_VAL_DIM), F32)
        for s in range(GLR_BLOCK):
            dec = jnp.exp(jnp.where(t_idx >= s, b3 - b3[:, s:s + 1, :], NEG_INF))
            a = jnp.sum(q3 * k3[:, s:s + 1, :] * dec, axis=-1, keepdims=True)
            o_in = o_in + a * v3[:, s:s + 1, :]

        v_t = bi.T
        lhs = jnp.where(in_block, v_t[None], 0.0).astype(BF16).reshape(nb * B_VAL_DIM, tt)
        u_t = _dot(lhs, k_dec).reshape(nb, B_VAL_DIM, B_KEY_DIM)
        decay_end = jnp.exp(b_end)
        st = st_ref[hh]
        outs = []
        for jb in range(nb):
            outs.append(_dot_nt(q_dec[jb], st.astype(BF16)))
            st = st * decay_end[jb] + u_t[jb]
        st_ref[hh] = st
        o = jnp.concatenate(outs, axis=0) + o_in.reshape(tt, B_VAL_DIM)
        o = _rms(o, gw) * (bo * jax.nn.sigmoid(bo))
        o_ref[:, B_VAL_DIM * hh:B_VAL_DIM * (hh + 1)] = o.astype(o_ref.dtype)

    @pl.when(it == pl.num_programs(1) - 1)
    def _():
        for hh in range(B_HEADS):
            s_ref[hh] = st_ref[hh].T.astype(s_ref.dtype)


def _glr(bg4, lb_logits, glr_norm_w, s0, t_valid, tt, layer):
    b, t, _ = bg4.shape
    state = pl.BlockSpec((None, B_HEADS, B_KEY_DIM, B_VAL_DIM), lambda bi, i: (bi, 0, 0, 0))
    return pl.pallas_call(
        functools.partial(_glr_kernel, t_valid, layer),
        grid=(b, t // tt),
        in_specs=[pl.BlockSpec((None, tt, IN_GROUP_B), lambda bi, i: (bi, i, 0)),
                  pl.BlockSpec(lb_logits.shape, lambda bi, i: (0, 0)),
                  pl.BlockSpec((1, B_VAL_DIM), lambda bi, i: (0, 0)),
                  state],
        out_specs=(pl.BlockSpec((None, tt, B_V), lambda bi, i: (bi, i, 0)), state),
        out_shape=(jax.ShapeDtypeStruct((b, t, B_V), BF16), jax.ShapeDtypeStruct(s0.shape, s0.dtype)),
        scratch_shapes=[pltpu.VMEM((B_HEADS, B_VAL_DIM, B_KEY_DIM), F32)],
        compiler_params=_params("parallel", "arbitrary"),
    )(bg4, lb_logits, glr_norm_w, s0)


def _mix_out_kernel(x_ref, att_ref, glr_ref, wa_ref, wb_ref, o_ref):
    o_ref[...] = x_ref[...] + _dot(att_ref[...], wa_ref[...]) + _dot(glr_ref[...], wb_ref[...])


def _mix_out(x, att, glr, w_a, w_b, tm):
    n, d = x.shape
    row = lambda width: pl.BlockSpec((tm, width), lambda i: (i, 0))
    return pl.pallas_call(
        _mix_out_kernel,
        grid=(n // tm,),
        in_specs=[row(d), row(A_Q), row(B_V), _const_spec(w_a.shape), _const_spec(w_b.shape)],
        out_specs=row(d),
        out_shape=jax.ShapeDtypeStruct((n, d), F32),
        compiler_params=_params("parallel"),
    )(x, att, glr, w_a, w_b)


def _pool_kernel(n_hist, x_ref, prev_ref, g_ref, wp_ref, sc_ref, o_ref, hist_ref, carry_ref):
    tt, d = x_ref.shape
    group = d // len(POOL_WINDOWS)
    it = pl.program_id(1)

    @pl.when(it == 0)
    def _():
        carry_ref[...] = prev_ref[...]

    x = x_ref[...]
    h = _rms(x, g_ref[...])
    ext = jnp.concatenate([carry_ref[...], h], axis=0)
    pos = it * tt + lax.broadcasted_iota(jnp.int32, (tt, 1), 0) + n_hist
    run = ext
    width = 1
    for gi, win in enumerate(POOL_WINDOWS):
        while width < win:
            run = run + pltpu.roll(run, width, 0)
            width *= 2
        sl = slice(group * gi, group * (gi + 1))
        cnt = jnp.minimum(pos + 1, win).astype(F32)
        z = run[POOL_HALO:, sl] / cnt - h[:, sl]
        y = _dot(z.astype(BF16), wp_ref[gi]) * sc_ref[:, sl]
        o_ref[:, sl] = x[:, sl] + y
    tail = ext[tt:, :]
    carry_ref[...] = tail
    hist_ref[...] = tail


def _pool(x, prev, g, w_pool, scale, n_hist, tt):
    b, t, d = x.shape
    hist_spec = pl.BlockSpec((None, POOL_HALO, d), lambda bi, i: (bi, 0, 0))
    return pl.pallas_call(
        functools.partial(_pool_kernel, n_hist),
        grid=(b, t // tt),
        in_specs=[pl.BlockSpec((None, tt, d), lambda bi, i: (bi, i, 0)), hist_spec,
                  pl.BlockSpec((1, d), lambda bi, i: (0, 0)),
                  pl.BlockSpec(w_pool.shape, lambda bi, i: (0, 0, 0)),
                  pl.BlockSpec((1, d), lambda bi, i: (0, 0))],
        out_specs=(pl.BlockSpec((None, tt, d), lambda bi, i: (bi, i, 0)), hist_spec),
        out_shape=(jax.ShapeDtypeStruct((b, t, d), F32), jax.ShapeDtypeStruct((b, POOL_HALO, d), F32)),
        scratch_shapes=[pltpu.VMEM((POOL_HALO, d), F32)],
        compiler_params=_params("parallel", "arbitrary"),
    )(x, prev, g, w_pool, scale)


def _ffn_ple_kernel(n_chunks, final, x_ref, p_ref, gf_ref, wg_ref, wu_ref, wd_ref, gp_ref, wpg_ref, wpe_ref,
                    gfin_ref, o_ref):
    x = x_ref[...]
    h = _rms(x, gf_ref[...]).astype(BF16)
    d_ff = wg_ref.shape[1]
    cw = d_ff // n_chunks
    acc = None
    for c in range(n_chunks):
        sl = slice(cw * c, cw * (c + 1))
        gate = _dot(h, wg_ref[:, sl])
        up = _dot(h, wu_ref[:, sl])
        part = _dot((gate * jax.nn.sigmoid(gate) * up).astype(BF16), wd_ref[sl, :])
        acc = part if acc is None else acc + part
    x = x + acc
    gate = jax.nn.sigmoid(_dot(_rms(x, gp_ref[...]).astype(BF16), wpg_ref[...]))
    x = x + _dot(p_ref[...].astype(BF16), wpe_ref[...]) * gate
    if final:
        x = _rms(x, gfin_ref[...])
    o_ref[...] = x


def _ffn_ple(x, p, g_ffn, w_gate, w_up, w_down, g_ple, w_ple_gate, w_ple, g_final, final, tm):
    n, d = x.shape
    d_ff = w_gate.shape[1]
    n_chunks = 2 if d_ff % (2 * LANES) == 0 else 1
    row = lambda width: pl.BlockSpec((tm, width), lambda i: (i, 0))
    return pl.pallas_call(
        functools.partial(_ffn_ple_kernel, n_chunks, final),
        grid=(n // tm,),
        in_specs=[row(d), row(p.shape[1]), _const_spec((1, d)), _const_spec(w_gate.shape), _const_spec(w_up.shape),
                  _const_spec(w_down.shape), _const_spec((1, d)), _const_spec(w_ple_gate.shape),
                  _const_spec(w_ple.shape), _const_spec((1, d))],
        out_specs=row(d),
        out_shape=jax.ShapeDtypeStruct((n, d), F32),
        compiler_params=_params("parallel"),
    )(x, p, g_ffn, w_gate, w_up, w_down, g_ple, w_ple_gate, w_ple, g_final)


def _rope_tables(pos):
    half = A_HEAD_DIM // 2
    inv = ROPE_THETA ** (-jnp.arange(half, dtype=F32) / half)
    ang = pos.astype(F32)[:, None] * inv[None, :]
    c, s = jnp.cos(ang), jnp.sin(ang)
    reps = LANES // A_HEAD_DIM
    return jnp.tile(jnp.concatenate([c, c], axis=1), (1, reps)), jnp.tile(jnp.concatenate([-s, s], axis=1), (1, reps))


def _pack_w_in(w):
    d = w.shape[0]
    n_a = 3 * A_Q + IDX_Q
    n_i = IDX_DIM + IDX_HEADS
    return jnp.concatenate([w[:, :n_a], w[:, n_a:n_a + n_i], jnp.zeros((d, IN_GROUP_I - n_i), w.dtype),
                            w[:, n_a + n_i:]], axis=1).astype(BF16)


def _row_tile(n, want):
    t = min(n, want)
    assert n % t == 0
    return t


def _trunk(x, p, pos, attend, glr_s0, glr_tile, pool_prev, pool_hist, W):
    b, t, d = x.shape
    n = b * t
    tm = _row_tile(n, 256)
    row2 = lambda a: a.reshape(1, -1)
    cos_tab, sin_tab = _rope_tables(pos)
    depth = W["mix_norm"].shape[0]
    xf = x.reshape(n, d)
    outs = {}
    for i in range(depth):
        jj = i // 2
        if i % 2 == 0:
            qb, kf, kb, vf, vb, qib, kiw, ki2, bg4 = _in_proj(
                xf, row2(W["mix_norm"][i]), _pack_w_in(W["w_in"][jj]), cos_tab, sin_tab, tm)
            att = attend(jj, qb, qib, kiw, ki2, kb, vb)
            t_pad = -(-t // glr_tile) * glr_tile
            bg3 = bg4.reshape(b, t, IN_GROUP_B)
            if t_pad != t:
                bg3 = jnp.pad(bg3, ((0, 0), (0, t_pad - t), (0, 0)))
            glr, s_new = _glr(bg3, W["lb_logits"], row2(W["glr_norm"][jj]), glr_s0[jj], t, glr_tile, jj)
            glr = glr[:, :t].reshape(n, B_V)
            w_out = W["w_out"][jj].astype(BF16)
            xf = _mix_out(xf, att, glr, w_out[:A_Q], w_out[A_Q:], tm)
            outs.setdefault("k", []).append(kf.reshape(b, t, A_HEADS, A_HEAD_DIM))
            outs.setdefault("v", []).append(vf.reshape(b, t, A_HEADS, A_HEAD_DIM))
            outs.setdefault("ki", []).append(kiw[:, :IDX_DIM].reshape(b, t, IDX_DIM))
            outs.setdefault("s", []).append(s_new)
        else:
            tt = _row_tile(t, 256)
            t_pad = -(-t // 8) * 8
            x3 = xf.reshape(b, t, d)
            if t_pad != t:
                x3 = jnp.pad(x3, ((0, 0), (0, t_pad - t), (0, 0)))
                tt = t_pad
            y3, hist = _pool(x3, pool_prev[jj], row2(W["mix_norm"][i]), W["pool_w"][jj].astype(BF16),
                             row2(W["pool_scale"][jj]), pool_hist, tt)
            xf = y3[:, :t].reshape(n, d)
            outs.setdefault("hist", []).append((hist, t_pad))
        d_ff = W["w_down"].shape[1]
        w_up = W["w_up"][i].astype(BF16)
        xf = _ffn_ple(xf, p[i].reshape(n, -1), row2(W["ffn_norm"][i]), w_up[:, :d_ff], w_up[:, d_ff:],
                      W["w_down"][i].astype(BF16), row2(W["ple_norm"][i]), W["w_ple_gate"][i].astype(BF16),
                      W["w_ple"][i].astype(BF16), row2(W["final_norm"]), i == depth - 1, tm)
    return xf.reshape(b, t, d), outs


def kernel(x_prompt, x_sample, cache_k, cache_v, cache_kidx, state_glr, state_pool, page_table, p_prompt, p_sample,
           mix_norm, w_in, w_out, lb_logits, glr_norm, pool_w, pool_scale, ffn_norm, w_up, w_down, ple_norm,
           w_ple_gate, w_ple, final_norm):
    W = dict(mix_norm=mix_norm, w_in=w_in, w_out=w_out, lb_logits=lb_logits, glr_norm=glr_norm, pool_w=pool_w,
             pool_scale=pool_scale, ffn_norm=ffn_norm, w_up=w_up, w_down=w_down, ple_norm=ple_norm,
             w_ple_gate=w_ple_gate, w_ple=w_ple, final_norm=final_norm)
    n_ab = w_in.shape[0]
    n_c = pool_w.shape[0]
    d = x_prompt.shape[-1]
    n_buf = max(POOL_WINDOWS) - 1

    bp, sp = x_prompt.shape[:2]

    def attend_prompt(jj, qb, qib, kiw, ki2, kb, vb):
        r3 = lambda a: a.reshape(bp, sp, a.shape[-1])
        tq = _row_tile(sp, 256)
        return _dsa_prompt(r3(qb), r3(qib), r3(kiw), r3(ki2), r3(kb), r3(vb), tq).reshape(bp * sp, A_Q)

    glr0_p = [jnp.zeros((bp, B_HEADS, B_KEY_DIM, B_VAL_DIM), state_glr.dtype) for _ in range(n_ab)]
    pool0_p = [jnp.zeros((bp, POOL_HALO, d), F32) for _ in range(n_c)]
    yp, op = _trunk(x_prompt, p_prompt, jnp.arange(sp, dtype=jnp.int32), attend_prompt, glr0_p,
                    _row_tile(sp, 256), pool0_p, 0, W)

    bd, ts = x_sample.shape[:2]
    past = page_table.shape[1] * PAGE_SIZE

    def attend_sample(jj, qb, qib, kiw, ki2, kb, vb):
        return _dsa_sample(qb, qib, kiw, ki2, kb, vb, cache_k[jj], cache_v[jj], cache_kidx[jj], page_table, ts)

    pos_s = jnp.tile(past + jnp.arange(ts, dtype=jnp.int32), bd)
    pool0_s = [jnp.pad(state_pool[j].astype(F32), ((0, 0), (POOL_HALO - n_buf, 0), (0, 0))) for j in range(n_c)]
    ys, os_ = _trunk(x_sample, p_sample, pos_s, attend_sample, [state_glr[j] for j in range(n_ab)],
                     LANES, pool0_s, n_buf, W)

    def pool_rows(o, prev, t):
        res = []
        for j, (hist, t_pad) in enumerate(o["hist"]):
            if t >= n_buf:
                res.append(hist[:, POOL_HALO - (t_pad - t) - n_buf:POOL_HALO - (t_pad - t)])
            else:
                new = hist[:, POOL_HALO - t_pad:POOL_HALO - t_pad + t]
                res.append(jnp.concatenate([prev[j][:, t:].astype(new.dtype), new], axis=1))
        return jnp.stack(res)

    return (yp, ys, jnp.stack(op["k"]), jnp.stack(op["v"]), jnp.stack(op["ki"]), jnp.stack(op["s"]),
            pool_rows(op, None, sp).astype(x_prompt.dtype),
            jnp.stack(os_["k"]), jnp.stack(os_["v"]), jnp.stack(os_["ki"]), jnp.stack(os_["s"]),
            pool_rows(os_, state_pool, ts).astype(x_sample.dtype))
```

```python
import functools

import jax
import jax.numpy as jnp
from jax import lax
from jax.experimental import pallas as pl
from jax.experimental.pallas import tpu as pltpu

F32 = jnp.float32
BF16 = jnp.bfloat16

A_HEADS = 8
A_HEAD_DIM = 64
IDX_HEADS = 8
IDX_DIM = 64
TOPK_MAX = 256
ROPE_THETA = 10000.0
B_HEADS = 4
B_KEY_DIM = 128
B_VAL_DIM = 128
GLR_BLOCK = 16
POOL_WINDOWS = (2, 4, 8, 16)
POOL_HALO = 16
PAGE_SIZE = 128
RMS_EPS = 1e-6

A_Q = A_HEADS * A_HEAD_DIM
IDX_Q = IDX_HEADS * IDX_DIM
B_K = B_HEADS * B_KEY_DIM
B_V = B_HEADS * B_VAL_DIM

LANES = 128
VMEM_LIMIT = 56 * 1024 * 1024
INT_MIN = -(2 ** 31)
NEG_INF = float("-inf")


def _params(*sem):
    return pltpu.CompilerParams(dimension_semantics=sem, vmem_limit_bytes=VMEM_LIMIT)


def _rms(x, g):
    ms = jnp.mean(x * x, axis=-1, keepdims=True)
    return x * lax.rsqrt(ms + RMS_EPS) * g


def _dot(a, b):
    return jnp.dot(a, b, preferred_element_type=F32)


def _dot_nt(a, b):
    return lax.dot_general(a, b, (((1,), (1,)), ((), ())), preferred_element_type=F32)


def _const_spec(shape):
    nd = len(shape)
    return pl.BlockSpec(shape, lambda *_: (0,) * nd, pipeline_mode=pl.Buffered(1))


IN_GROUP_A = 4 * A_Q
IN_GROUP_I = LANES
IN_GROUP_B = 2 * B_K + 2 * B_V
IN_PACKED = IN_GROUP_A + IN_GROUP_I + IN_GROUP_B


def _in_proj_kernel(x_ref, g_ref, w_ref, cos_ref, sin_ref,
                    qb_ref, kf_ref, kb_ref, vf_ref, vb_ref, qib_ref, kiw_ref, ki2_ref, bg_ref):
    tm = x_ref.shape[0]
    h = _rms(x_ref[...], g_ref[...]).astype(BF16)
    cos = cos_ref[...]
    sin = sin_ref[...]
    lane = lax.broadcasted_iota(jnp.int32, (tm, LANES), 1)
    first = (lane % A_HEAD_DIM) < (A_HEAD_DIM // 2)

    def rope(z, c, s):
        rot = jnp.where(first, pltpu.roll(z, LANES - A_HEAD_DIM // 2, 1), pltpu.roll(z, A_HEAD_DIM // 2, 1))
        return z * c + rot * s

    def roped(col0):
        z = _dot(h, w_ref[:, col0:col0 + A_Q])
        return [rope(z[:, LANES * i:LANES * (i + 1)], cos, sin) for i in range(A_Q // LANES)]

    scale = A_HEAD_DIM ** -0.5
    for i, r in enumerate(roped(0)):
        qb_ref[:, LANES * i:LANES * (i + 1)] = (r * scale).astype(BF16)
    for i, r in enumerate(roped(A_Q)):
        kf_ref[:, LANES * i:LANES * (i + 1)] = r
        kb_ref[:, LANES * i:LANES * (i + 1)] = r.astype(BF16)
    v = _dot(h, w_ref[:, 2 * A_Q:3 * A_Q])
    vf_ref[...] = v
    vb_ref[...] = v.astype(BF16)
    for i, r in enumerate(roped(3 * A_Q)):
        qib_ref[:, LANES * i:LANES * (i + 1)] = (r * (IDX_DIM ** -0.5)).astype(BF16)
    is_key = lane < IDX_DIM
    z = _dot(h, w_ref[:, IN_GROUP_A:IN_GROUP_A + IN_GROUP_I])
    kiw = rope(z, jnp.where(is_key, cos, 1.0), jnp.where(is_key, sin, 0.0))
    kiw_ref[...] = kiw
    ki2_ref[...] = jnp.where(is_key, kiw, pltpu.roll(kiw, IDX_DIM, 1)).astype(BF16)
    col0 = IN_GROUP_A + IN_GROUP_I
    for i in range(IN_GROUP_B // A_Q):
        bg_ref[:, A_Q * i:A_Q * (i + 1)] = _dot(h, w_ref[:, col0 + A_Q * i:col0 + A_Q * (i + 1)])


def _in_proj(x, g, w_packed, cos_tab, sin_tab, tm):
    n, d = x.shape
    n_tab = cos_tab.shape[0] // tm
    row = lambda width: pl.BlockSpec((tm, width), lambda i: (i, 0))
    out_shape = (
        jax.ShapeDtypeStruct((n, A_Q), BF16),
        jax.ShapeDtypeStruct((n, A_Q), F32),
        jax.ShapeDtypeStruct((n, A_Q), BF16),
        jax.ShapeDtypeStruct((n, A_Q), F32),
        jax.ShapeDtypeStruct((n, A_Q), BF16),
        jax.ShapeDtypeStruct((n, IDX_Q), BF16),
        jax.ShapeDtypeStruct((n, LANES), F32),
        jax.ShapeDtypeStruct((n, LANES), BF16),
        jax.ShapeDtypeStruct((n, IN_GROUP_B), F32),
    )
    return pl.pallas_call(
        _in_proj_kernel,
        grid=(n // tm,),
        in_specs=[row(d), _const_spec((1, d)), _const_spec((d, IN_PACKED)),
                  pl.BlockSpec((tm, LANES), lambda i: (i % n_tab, 0)),
                  pl.BlockSpec((tm, LANES), lambda i: (i % n_tab, 0))],
        out_specs=tuple(row(s.shape[1]) for s in out_shape),
        out_shape=out_shape,
        compiler_params=_params("parallel"),
    )(x, g, w_packed, cos_tab, sin_tab)


def _order_key(x):
    bits = pltpu.bitcast(x + 0.0, jnp.int32)
    return bits ^ ((bits >> 31) & jnp.int32(0x7FFFFFFF))


def _count(get_key, n_c, pred):
    acc = None
    for c in range(n_c):
        t = jnp.where(pred(get_key(c)), 1.0, 0.0)
        acc = t if acc is None else acc + t
    return jnp.sum(acc, axis=1, keepdims=True)


def _select_bias(n_c, get_key, get_bias, set_bias, kk, use_topk):
    rows = use_topk.shape[0]
    kf = float(kk)

    def body(i, t):
        cand = t | jnp.left_shift(jnp.int32(1), 31 - i)
        signed = cand ^ jnp.int32(INT_MIN)
        return jnp.where(_count(get_key, n_c, lambda k: k >= signed) >= kf, cand, t)

    thr = lax.fori_loop(0, 32, body, jnp.zeros((rows, 1), jnp.int32)) ^ jnp.int32(INT_MIN)
    cnt_ge = None
    for c in range(n_c):
        ge = get_key(c) >= thr
        t = jnp.where(ge, 1.0, 0.0)
        cnt_ge = t if cnt_ge is None else cnt_ge + t
        set_bias(c, jnp.where(use_topk, jnp.where(ge, 0.0, NEG_INF), get_bias(c)))
    cnt_ge = jnp.sum(cnt_ge, axis=1, keepdims=True)
    tied = jnp.logical_and(use_topk, cnt_ge != kf)

    @pl.when(jnp.max(jnp.where(tied, 1.0, 0.0)) > 0.0)
    def _():
        room = kf - _count(get_key, n_c, lambda k: k > thr)
        before = (lax.broadcasted_iota(jnp.int32, (LANES, LANES), 0)
                  < lax.broadcasted_iota(jnp.int32, (LANES, LANES), 1))
        tri = jnp.where(before, 1.0, 0.0).astype(BF16)
        seen = jnp.zeros((rows, 1), F32)
        for c in range(n_c):
            kc = get_key(c)
            eq = jnp.where(kc == thr, 1.0, 0.0)
            rank = _dot(eq.astype(BF16), tri) + seen
            keep = jnp.logical_or(kc > thr, jnp.logical_and(kc == thr, rank < room))
            set_bias(c, jnp.where(use_topk, jnp.where(keep, 0.0, NEG_INF), get_bias(c)))
            seen = seen + jnp.sum(eq, axis=1, keepdims=True)


def _pair_masks(rows):
    lane = lax.broadcasted_iota(jnp.int32, (rows, LANES), 1)
    lo = lane < A_HEAD_DIM
    return lo, jnp.logical_not(lo)


def _index_scores(qi_ref, ki2, w):
    rows = qi_ref.shape[0]
    masks = _pair_masks(rows)
    acc = None
    for p in range(IDX_Q // LANES):
        slab = qi_ref[:, LANES * p:LANES * (p + 1)]
        for half in range(2):
            s = _dot_nt(jnp.where(masks[half], slab, jnp.zeros_like(slab)), ki2)
            hh = 2 * p + half
            term = jnp.maximum(s, 0.0) * w[:, hh:hh + 1]
            acc = term if acc is None else acc + term
    return acc


def _masked_attention(q_ref, k_ref, v_ref, bias, o_ref):
    rows = q_ref.shape[0]
    ext = bias.shape[1]
    masks = _pair_masks(rows)
    for p in range(A_Q // LANES):
        sl = slice(LANES * p, LANES * (p + 1))
        slab = q_ref[:, sl]
        ks = k_ref[0:ext, sl]
        vs = v_ref[0:ext, sl]
        outs = []
        for half in range(2):
            s = _dot_nt(jnp.where(masks[half], slab, jnp.zeros_like(slab)), ks) + bias
            m = jnp.max(s, axis=1, keepdims=True)
            e = jnp.exp(s - m)
            l = jnp.sum(e, axis=1, keepdims=True)
            outs.append(_dot(e.astype(BF16), vs) / l)
        o_ref[:, sl] = jnp.where(masks[0], outs[0], outs[1]).astype(o_ref.dtype)


def _dsa_prompt_block(kk, jj, q_ref, qi_ref, kiw_ref, ki2_ref, k_ref, v_ref, o_ref, key_ref, bias_ref):
    tq = q_ref.shape[0]
    ext = (jj + 1) * tq
    n_c = ext // LANES
    first_diag = jj * tq // LANES
    qpos = jj * tq + lax.broadcasted_iota(jnp.int32, (tq, 1), 0)
    chunk = lambda c: (slice(None), slice(LANES * c, LANES * (c + 1)))
    causal = lambda c: LANES * c + lax.broadcasted_iota(jnp.int32, (1, LANES), 1) <= qpos

    def set_bias(c, val):
        bias_ref[chunk(c)] = val

    for c in range(n_c):
        set_bias(c, jnp.zeros((tq, LANES), F32) if c < first_diag else jnp.where(causal(c), 0.0, NEG_INF))
    if ext > kk:
        w = kiw_ref[:, IDX_DIM:IDX_DIM + IDX_HEADS] * (IDX_HEADS ** -0.5)
        scores = _index_scores(qi_ref, ki2_ref[0:ext, :], w)
        for c in range(n_c):
            sc = scores[chunk(c)]
            key_ref[chunk(c)] = _order_key(sc if c < first_diag else jnp.where(causal(c), sc, NEG_INF))
        _select_bias(n_c, lambda c: key_ref[chunk(c)], lambda c: bias_ref[chunk(c)], set_bias, kk, qpos >= kk)
    _masked_attention(q_ref, k_ref, v_ref, bias_ref[:, 0:ext], o_ref)


def _dsa_prompt_kernel(kk, *refs):
    q_ref, k_ref = refs[0], refs[4]
    j = pl.program_id(1)
    for jj in range(k_ref.shape[0] // q_ref.shape[0]):
        @pl.when(j == jj)
        def _(jj=jj):
            _dsa_prompt_block(kk, jj, *refs)


def _dsa_prompt(qb, qib, kiw, ki2, kb, vb, tq):
    b, s, _ = qb.shape
    kk = min(TOPK_MAX, s // 4)
    tile = lambda width: pl.BlockSpec((None, tq, width), lambda bi, j: (bi, j, 0))
    full = lambda width: pl.BlockSpec((None, s, width), lambda bi, j: (bi, 0, 0))
    return pl.pallas_call(
        functools.partial(_dsa_prompt_kernel, kk),
        grid=(b, s // tq),
        in_specs=[tile(A_Q), tile(IDX_Q), tile(LANES), full(LANES), full(A_Q), full(A_Q)],
        out_specs=tile(A_Q),
        out_shape=jax.ShapeDtypeStruct((b, s, A_Q), BF16),
        scratch_shapes=[pltpu.VMEM((tq, s), jnp.int32), pltpu.VMEM((tq, s), F32)],
        compiler_params=_params("parallel", "arbitrary"),
    )(qb, qib, kiw, ki2, kb, vb)


TPAD = 8
NEW_PAD = LANES
CHUNK_PAGES = 16


def _dsa_sample_kernel(t_new, n_pages, kk, pt_ref, qbd_ref, qir_ref, wr_ref, knew_ref, vnew_ref, kinew_ref,
                       ckidx_ref, ck_ref, cv_ref, o_ref,
                       kibuf, kvbuf, logit_ref, key_ref, bias_ref, sem_ki, sem_kv):
    b = pl.program_id(0)
    past = n_pages * PAGE_SIZE
    rows = A_HEADS * TPAD
    n_chunks = n_pages // CHUNK_PAGES

    def ki_copy(p):
        return pltpu.make_async_copy(ckidx_ref.at[pt_ref[b, p]], kibuf.at[p], sem_ki)

    def kv_copy(src_ref, c, i, slot):
        return pltpu.make_async_copy(src_ref.at[pt_ref[b, c * CHUNK_PAGES + i]], kvbuf.at[slot, i], sem_kv.at[slot])

    def start_chunk(src_ref, c, slot):
        lax.fori_loop(0, CHUNK_PAGES, lambda i, _: (kv_copy(src_ref, c, i, slot).start(), 0)[1], 0)

    def wait_chunk(src_ref, c, slot):
        lax.fori_loop(0, CHUNK_PAGES, lambda i, _: (kv_copy(src_ref, c, i, slot).wait(), 0)[1], 0)

    lax.fori_loop(0, n_pages, lambda p, _: (ki_copy(p).start(), 0)[1], 0)
    start_chunk(ck_ref, 0, 0)
    lax.fori_loop(0, n_pages, lambda p, _: (ki_copy(p).wait(), 0)[1], 0)

    qir = qir_ref[...]
    w = wr_ref[...] * (IDX_HEADS ** -0.5)
    tok = lax.broadcasted_iota(jnp.int32, (TPAD, 1), 0)
    tok_ok = tok < t_new

    def keys_of(s, ok):
        t = jnp.maximum(s, 0.0) * w
        return _order_key(jnp.where(ok, jnp.sum(t.reshape(IDX_HEADS, TPAD, LANES), axis=0), NEG_INF))

    def index_page(p, _):
        key_ref[p] = keys_of(_dot(qir, kibuf[p].astype(BF16)), tok_ok)
        return 0

    lax.fori_loop(0, n_pages, index_page, 0, unroll=4)
    new_idx = lax.broadcasted_iota(jnp.int32, (1, NEW_PAD), 1)
    new_ok = jnp.logical_and(jnp.logical_and(new_idx <= tok, new_idx < t_new), tok_ok)
    key_ref[n_pages] = keys_of(_dot_nt(qir, kinew_ref[...]), new_ok)
    bias_ref[0:n_pages] = jnp.broadcast_to(jnp.where(tok_ok, 0.0, NEG_INF)[None], (n_pages, TPAD, LANES))
    bias_ref[n_pages] = jnp.where(new_ok, 0.0, NEG_INF)

    def set_bias(c, val):
        bias_ref[c] = val

    use_topk = jnp.logical_and(past + tok + 1 > kk, tok_ok)
    _select_bias(n_pages + 1, lambda c: key_ref[c], lambda c: bias_ref[c], set_bias, kk, use_topk)

    qbd = qbd_ref[...]
    for c in range(n_chunks):
        slot = c % 2
        wait_chunk(ck_ref, c, slot)
        if c + 1 < n_chunks:
            start_chunk(ck_ref, c + 1, 1 - slot)
        else:
            start_chunk(cv_ref, 0, 1 - slot)

        def logits_page(i, _, c=c, slot=slot):
            logit_ref[c * CHUNK_PAGES + i] = _dot(qbd, kvbuf[slot, i].astype(BF16))
            return 0

        lax.fori_loop(0, CHUNK_PAGES, logits_page, 0, unroll=2)
    logit_ref[n_pages] = _dot_nt(qbd, knew_ref[...])
    s = logit_ref[...].reshape(n_pages + 1, A_HEADS, TPAD, LANES) + bias_ref[...][:, None]
    s = jnp.where(tok_ok[None, None], s, 0.0).reshape(n_pages + 1, rows, LANES)
    m = jnp.max(jnp.max(s, axis=0), axis=1, keepdims=True)
    e = jnp.exp(s - m[None])
    l = jnp.sum(jnp.sum(e, axis=0), axis=1, keepdims=True)
    logit_ref[...] = e

    acc = _dot(logit_ref[n_pages].astype(BF16), vnew_ref[...])
    for c in range(n_chunks):
        slot = (n_chunks + c) % 2
        wait_chunk(cv_ref, c, slot)
        if c + 1 < n_chunks:
            start_chunk(cv_ref, c + 1, 1 - slot)

        def value_page(i, a, c=c, slot=slot):
            return a + _dot_nt(logit_ref[c * CHUNK_PAGES + i].astype(BF16), kvbuf[slot, i].astype(BF16))

        acc = lax.fori_loop(0, CHUNK_PAGES, value_page, acc, unroll=2)
    acc = (acc / l).reshape(A_HEADS, TPAD, A_Q)
    head_of_lane = lax.broadcasted_iota(jnp.int32, (TPAD, A_Q), 1) // A_HEAD_DIM
    out = jnp.zeros((TPAD, A_Q), F32)
    for hh in range(A_HEADS):
        out = out + jnp.where(head_of_lane == hh, acc[hh], 0.0)
    o_ref[...] = out.astype(o_ref.dtype)


def _dsa_sample(qb, qib, kiw, ki2, kb, vb, cache_kt, cache_vt, cache_kit, page_table, t_new):
    bd, n_pages = page_table.shape
    assert n_pages % CHUNK_PAGES == 0 and t_new <= TPAD
    past = n_pages * PAGE_SIZE
    kk = min(TOPK_MAX, (past + t_new) // 4)
    rows = A_HEADS * TPAD

    def pad_tokens(a, to):
        a = a.reshape(bd, t_new, a.shape[-1])
        return jnp.pad(a, ((0, 0), (0, to - t_new), (0, 0)))

    q8 = pad_tokens(qb, TPAD)
    head_of_lane = jnp.arange(A_Q) // A_HEAD_DIM
    qbd = jnp.where(head_of_lane[None, None, None, :] == jnp.arange(A_HEADS)[None, :, None, None],
                    q8[:, None], jnp.zeros((), BF16)).reshape(bd, rows, A_Q)
    qir = pad_tokens(qib, TPAD).reshape(bd, TPAD, IDX_HEADS, IDX_DIM).transpose(0, 2, 1, 3).reshape(bd, rows, IDX_DIM)
    wr = pad_tokens(kiw[:, IDX_DIM:IDX_DIM + IDX_HEADS], TPAD).transpose(0, 2, 1).reshape(bd, rows, 1)
    knew = pad_tokens(kb, NEW_PAD)
    vnew = pad_tokens(vb, NEW_PAD)
    kinew = pad_tokens(ki2[:, :IDX_DIM], NEW_PAD)

    per_seq = lambda r, c: pl.BlockSpec((None, r, c), lambda b, pt: (b, 0, 0))
    any_spec = pl.BlockSpec(memory_space=pl.ANY)
    grid_spec = pltpu.PrefetchScalarGridSpec(
        num_scalar_prefetch=1,
        grid=(bd,),
        in_specs=[per_seq(rows, A_Q), per_seq(rows, IDX_DIM), per_seq(rows, 1), per_seq(NEW_PAD, A_Q),
                  per_seq(NEW_PAD, A_Q), per_seq(NEW_PAD, IDX_DIM), any_spec, any_spec, any_spec],
        out_specs=per_seq(TPAD, A_Q),
        scratch_shapes=[
            pltpu.VMEM((n_pages, IDX_DIM, PAGE_SIZE), F32),
            pltpu.VMEM((2, CHUNK_PAGES, A_Q, PAGE_SIZE), F32),
            pltpu.VMEM((n_pages + 1, rows, LANES), F32),
            pltpu.VMEM((n_pages + 1, TPAD, LANES), jnp.int32),
            pltpu.VMEM((n_pages + 1, TPAD, LANES), F32),
            pltpu.SemaphoreType.DMA(()),
            pltpu.SemaphoreType.DMA((2,)),
        ],
    )
    out = pl.pallas_call(
        functools.partial(_dsa_sample_kernel, t_new, n_pages, kk),
        grid_spec=grid_spec,
        out_shape=jax.ShapeDtypeStruct((bd, TPAD, A_Q), BF16),
        compiler_params=_params("arbitrary"),
    )(page_table, qbd, qir, wr, knew, vnew, kinew, cache_kit, cache_kt, cache_vt)
    return out[:, :t_new].reshape(bd * t_new, A_Q)


def _glr_kernel(t_valid, layer, bg_ref, lbl_ref, gw_ref, s0_ref, o_ref, s_ref, st_ref):
    tt = bg_ref.shape[0]
    nb = tt // GLR_BLOCK
    it = pl.program_id(1)

    @pl.when(it == 0)
    def _():
        for hh in range(B_HEADS):
            st_ref[hh] = s0_ref[hh].astype(F32).T

    logits = lbl_ref[...]
    ex = jnp.exp(logits - jnp.max(logits, axis=0, keepdims=True))
    lb_all = jnp.sum(ex[:layer + 1], axis=0, keepdims=True) / jnp.sum(ex, axis=0, keepdims=True)

    row = lax.broadcasted_iota(jnp.int32, (tt, 1), 0)
    valid = (it * tt + row) < t_valid
    rloc = row % GLR_BLOCK
    t_idx = lax.broadcasted_iota(jnp.int32, (1, GLR_BLOCK, 1), 1)
    blk_of_col = lax.broadcasted_iota(jnp.int32, (nb, 1, tt), 2) // GLR_BLOCK
    blk_id = lax.broadcasted_iota(jnp.int32, (nb, 1, tt), 0)
    in_block = blk_of_col == blk_id
    gw = gw_ref[...]

    for hh in range(B_HEADS):
        lb = lb_all[:, B_KEY_DIM * hh:B_KEY_DIM * (hh + 1)]
        bq = bg_ref[:, B_KEY_DIM * hh:B_KEY_DIM * (hh + 1)]
        bf = bg_ref[:, B_K + B_KEY_DIM * hh:B_K + B_KEY_DIM * (hh + 1)]
        bi = bg_ref[:, 2 * B_K + B_VAL_DIM * hh:2 * B_K + B_VAL_DIM * (hh + 1)]
        bo = bg_ref[:, 2 * B_K + B_V + B_VAL_DIM * hh:2 * B_K + B_V + B_VAL_DIM * (hh + 1)]
        f = lb + (1.0 - lb) * jax.nn.sigmoid(bf)
        kx = jnp.where(valid, (1.0 - lb) * jax.nn.sigmoid(-bf), 0.0)
        qx = bq * jax.nn.sigmoid(bq)
        cum = jnp.where(valid, jnp.log(f), 0.0)
        for sh in (1, 2, 4, 8):
            cum = cum + jnp.where(rloc >= sh, pltpu.roll(cum, sh, 0), 0.0)
        b3 = cum.reshape(nb, GLR_BLOCK, B_KEY_DIM)
        q3 = qx.reshape(nb, GLR_BLOCK, B_KEY_DIM)
        k3 = kx.reshape(nb, GLR_BLOCK, B_KEY_DIM)
        v3 = bi.reshape(nb, GLR_BLOCK, B_VAL_DIM)
        b_end = b3[:, GLR_BLOCK - 1:GLR_BLOCK, :]
        q_dec = (q3 * jnp.exp(b3)).astype(BF16)
        k_dec = (k3 * jnp.exp(b_end - b3)).reshape(tt, B_KEY_DIM).astype(BF16)

        o_in = jnp.zeros((nb, GLR_BLOCK, B_VAL_DIM), F32)
        for s in range(GLR_BLOCK):
            dec = jnp.exp(jnp.where(t_idx >= s, b3 - b3[:, s:s + 1, :], NEG_INF))
            a = jnp.sum(q3 * k3[:, s:s + 1, :] * dec, axis=-1, keepdims=True)
            o_in = o_in + a * v3[:, s:s + 1, :]

        v_t = bi.T
        lhs = jnp.where(in_block, v_t[None], 0.0).astype(BF16).reshape(nb * B_VAL_DIM, tt)
        u_t = _dot(lhs, k_dec).reshape(nb, B_VAL_DIM, B_KEY_DIM)
        decay_end = jnp.exp(b_end)
        st = st_ref[hh]
        outs = []
        for jb in range(nb):
            outs.append(_dot_nt(q_dec[jb], st.astype(BF16)))
            st = st * decay_end[jb] + u_t[jb]
        st_ref[hh] = st
        o = jnp.concatenate(outs, axis=0) + o_in.reshape(tt, B_VAL_DIM)
        o = _rms(o, gw) * (bo * jax.nn.sigmoid(bo))
        o_ref[:, B_VAL_DIM * hh:B_VAL_DIM * (hh + 1)] = o.astype(o_ref.dtype)

    @pl.when(it == pl.num_programs(1) - 1)
    def _():
        for hh in range(B_HEADS):
            s_ref[hh] = st_ref[hh].T.astype(s_ref.dtype)


def _glr(bg4, lb_logits, glr_norm_w, s0, t_valid, tt, layer):
    b, t, _ = bg4.shape
    state = pl.BlockSpec((None, B_HEADS, B_KEY_DIM, B_VAL_DIM), lambda bi, i: (bi, 0, 0, 0))
    return pl.pallas_call(
        functools.partial(_glr_kernel, t_valid, layer),
        grid=(b, t // tt),
        in_specs=[pl.BlockSpec((None, tt, IN_GROUP_B), lambda bi, i: (bi, i, 0)),
                  pl.BlockSpec(lb_logits.shape, lambda bi, i: (0, 0)),
                  pl.BlockSpec((1, B_VAL_DIM), lambda bi, i: (0, 0)),
                  state],
        out_specs=(pl.BlockSpec((None, tt, B_V), lambda bi, i: (bi, i, 0)), state),
        out_shape=(jax.ShapeDtypeStruct((b, t, B_V), BF16), jax.ShapeDtypeStruct(s0.shape, s0.dtype)),
        scratch_shapes=[pltpu.VMEM((B_HEADS, B_VAL_DIM, B_KEY_DIM), F32)],
        compiler_params=_params("parallel", "arbitrary"),
    )(bg4, lb_logits, glr_norm_w, s0)


def _mix_out_kernel(x_ref, att_ref, glr_ref, wa_ref, wb_ref, o_ref):
    o_ref[...] = x_ref[...] + _dot(att_ref[...], wa_ref[...]) + _dot(glr_ref[...], wb_ref[...])


def _mix_out(x, att, glr, w_a, w_b, tm):
    n, d = x.shape
    row = lambda width: pl.BlockSpec((tm, width), lambda i: (i, 0))
    return pl.pallas_call(
        _mix_out_kernel,
        grid=(n // tm,),
        in_specs=[row(d), row(A_Q), row(B_V), _const_spec(w_a.shape), _const_spec(w_b.shape)],
        out_specs=row(d),
        out_shape=jax.ShapeDtypeStruct((n, d), F32),
        compiler_params=_params("parallel"),
    )(x, att, glr, w_a, w_b)


def _pool_kernel(n_hist, x_ref, prev_ref, g_ref, wp_ref, sc_ref, o_ref, hist_ref, carry_ref):
    tt, d = x_ref.shape
    group = d // len(POOL_WINDOWS)
    it = pl.program_id(1)

    @pl.when(it == 0)
    def _():
        carry_ref[...] = prev_ref[...]

    x = x_ref[...]
    h = _rms(x, g_ref[...])
    ext = jnp.concatenate([carry_ref[...], h], axis=0)
    pos = it * tt + lax.broadcasted_iota(jnp.int32, (tt, 1), 0) + n_hist
    run = ext
    width = 1
    for gi, win in enumerate(POOL_WINDOWS):
        while width < win:
            run = run + pltpu.roll(run, width, 0)
            width *= 2
        sl = slice(group * gi, group * (gi + 1))
        cnt = jnp.minimum(pos + 1, win).astype(F32)
        z = run[POOL_HALO:, sl] / cnt - h[:, sl]
        y = _dot(z.astype(BF16), wp_ref[gi]) * sc_ref[:, sl]
        o_ref[:, sl] = x[:, sl] + y
    tail = ext[tt:, :]
    carry_ref[...] = tail
    hist_ref[...] = tail


def _pool(x, prev, g, w_pool, scale, n_hist, tt):
    b, t, d = x.shape
    hist_spec = pl.BlockSpec((None, POOL_HALO, d), lambda bi, i: (bi, 0, 0))
    return pl.pallas_call(
        functools.partial(_pool_kernel, n_hist),
        grid=(b, t // tt),
        in_specs=[pl.BlockSpec((None, tt, d), lambda bi, i: (bi, i, 0)), hist_spec,
                  pl.BlockSpec((1, d), lambda bi, i: (0, 0)),
                  pl.BlockSpec(w_pool.shape, lambda bi, i: (0, 0, 0)),
                  pl.BlockSpec((1, d), lambda bi, i: (0, 0))],
        out_specs=(pl.BlockSpec((None, tt, d), lambda bi, i: (bi, i, 0)), hist_spec),
        out_shape=(jax.ShapeDtypeStruct((b, t, d), F32), jax.ShapeDtypeStruct((b, POOL_HALO, d), F32)),
        scratch_shapes=[pltpu.VMEM((POOL_HALO, d), F32)],
        compiler_params=_params("parallel", "arbitrary"),
    )(x, prev, g, w_pool, scale)


def _ffn_ple_kernel(n_chunks, final, x_ref, p_ref, gf_ref, wg_ref, wu_ref, wd_ref, gp_ref, wpg_ref, wpe_ref,
                    gfin_ref, o_ref):
    x = x_ref[...]
    h = _rms(x, gf_ref[...]).astype(BF16)
    d_ff = wg_ref.shape[1]
    cw = d_ff // n_chunks
    acc = None
    for c in range(n_chunks):
        sl = slice(cw * c, cw * (c + 1))
        gate = _dot(h, wg_ref[:, sl])
        up = _dot(h, wu_ref[:, sl])
        part = _dot((gate * jax.nn.sigmoid(gate) * up).astype(BF16), wd_ref[sl, :])
        acc = part if acc is None else acc + part
    x = x + acc
    gate = jax.nn.sigmoid(_dot(_rms(x, gp_ref[...]).astype(BF16), wpg_ref[...]))
    x = x + _dot(p_ref[...].astype(BF16), wpe_ref[...]) * gate
    if final:
        x = _rms(x, gfin_ref[...])
    o_ref[...] = x


def _ffn_ple(x, p, g_ffn, w_gate, w_up, w_down, g_ple, w_ple_gate, w_ple, g_final, final, tm):
    n, d = x.shape
    d_ff = w_gate.shape[1]
    n_chunks = 2 if d_ff % (2 * LANES) == 0 else 1
    row = lambda width: pl.BlockSpec((tm, width), lambda i: (i, 0))
    return pl.pallas_call(
        functools.partial(_ffn_ple_kernel, n_chunks, final),
        grid=(n // tm,),
        in_specs=[row(d), row(p.shape[1]), _const_spec((1, d)), _const_spec(w_gate.shape), _const_spec(w_up.shape),
                  _const_spec(w_down.shape), _const_spec((1, d)), _const_spec(w_ple_gate.shape),
                  _const_spec(w_ple.shape), _const_spec((1, d))],
        out_specs=row(d),
        out_shape=jax.ShapeDtypeStruct((n, d), F32),
        compiler_params=_params("parallel"),
    )(x, p, g_ffn, w_gate, w_up, w_down, g_ple, w_ple_gate, w_ple, g_final)


def _rope_tables(pos):
    half = A_HEAD_DIM // 2
    inv = ROPE_THETA ** (-jnp.arange(half, dtype=F32) / half)
    ang = pos.astype(F32)[:, None] * inv[None, :]
    c, s = jnp.cos(ang), jnp.sin(ang)
    reps = LANES // A_HEAD_DIM
    return jnp.tile(jnp.concatenate([c, c], axis=1), (1, reps)), jnp.tile(jnp.concatenate([-s, s], axis=1), (1, reps))


def _pack_w_in(w):
    d = w.shape[0]
    n_a = 3 * A_Q + IDX_Q
    n_i = IDX_DIM + IDX_HEADS
    return jnp.concatenate([w[:, :n_a], w[:, n_a:n_a + n_i], jnp.zeros((d, IN_GROUP_I - n_i), w.dtype),
                            w[:, n_a + n_i:]], axis=1).astype(BF16)


def _row_tile(n, want):
    t = min(n, want)
    assert n % t == 0
    return t


def _trunk(x, p, pos, attend, glr_s0, glr_tile, pool_prev, pool_hist, W):
    b, t, d = x.shape
    n = b * t
    tm = _row_tile(n, 256)
    row2 = lambda a: a.reshape(1, -1)
    cos_tab, sin_tab = _rope_tables(pos)
    depth = W["mix_norm"].shape[0]
    xf = x.reshape(n, d)
    outs = {}
    for i in range(depth):
        jj = i // 2
        if i % 2 == 0:
            qb, kf, kb, vf, vb, qib, kiw, ki2, bg4 = _in_proj(
                xf, row2(W["mix_norm"][i]), _pack_w_in(W["w_in"][jj]), cos_tab, sin_tab, tm)
            att = attend(jj, qb, qib, kiw, ki2, kb, vb)
            t_pad = -(-t // glr_tile) * glr_tile
            bg3 = bg4.reshape(b, t, IN_GROUP_B)
            if t_pad != t:
                bg3 = jnp.pad(bg3, ((0, 0), (0, t_pad - t), (0, 0)))
            glr, s_new = _glr(bg3, W["lb_logits"], row2(W["glr_norm"][jj]), glr_s0[jj], t, glr_tile, jj)
            glr = glr[:, :t].reshape(n, B_V)
            w_out = W["w_out"][jj].astype(BF16)
            xf = _mix_out(xf, att, glr, w_out[:A_Q], w_out[A_Q:], tm)
            outs.setdefault("k", []).append(kf.reshape(b, t, A_HEADS, A_HEAD_DIM))
            outs.setdefault("v", []).append(vf.reshape(b, t, A_HEADS, A_HEAD_DIM))
            outs.setdefault("ki", []).append(kiw[:, :IDX_DIM].reshape(b, t, IDX_DIM))
            outs.setdefault("s", []).append(s_new)
        else:
            tt = _row_tile(t, 256)
            t_pad = -(-t // 8) * 8
            x3 = xf.reshape(b, t, d)
            if t_pad != t:
                x3 = jnp.pad(x3, ((0, 0), (0, t_pad - t), (0, 0)))
                tt = t_pad
            y3, hist = _pool(x3, pool_prev[jj], row2(W["mix_norm"][i]), W["pool_w"][jj].astype(BF16),
                             row2(W["pool_scale"][jj]), pool_hist, tt)
            xf = y3[:, :t].reshape(n, d)
            outs.setdefault("hist", []).append((hist, t_pad))
        d_ff = W["w_down"].shape[1]
        w_up = W["w_up"][i].astype(BF16)
        xf = _ffn_ple(xf, p[i].reshape(n, -1), row2(W["ffn_norm"][i]), w_up[:, :d_ff], w_up[:, d_ff:],
                      W["w_down"][i].astype(BF16), row2(W["ple_norm"][i]), W["w_ple_gate"][i].astype(BF16),
                      W["w_ple"][i].astype(BF16), row2(W["final_norm"]), i == depth - 1, tm)
    return xf.reshape(b, t, d), outs


def kernel(x_prompt, x_sample, cache_k, cache_v, cache_kidx, state_glr, state_pool, page_table, p_prompt, p_sample,
           mix_norm, w_in, w_out, lb_logits, glr_norm, pool_w, pool_scale, ffn_norm, w_up, w_down, ple_norm,
           w_ple_gate, w_ple, final_norm):
    W = dict(mix_norm=mix_norm, w_in=w_in, w_out=w_out, lb_logits=lb_logits, glr_norm=glr_norm, pool_w=pool_w,
             pool_scale=pool_scale, ffn_norm=ffn_norm, w_up=w_up, w_down=w_down, ple_norm=ple_norm,
             w_ple_gate=w_ple_gate, w_ple=w_ple, final_norm=final_norm)
    n_ab = w_in.shape[0]
    n_c = pool_w.shape[0]
    d = x_prompt.shape[-1]
    n_buf = max(POOL_WINDOWS) - 1

    bp, sp = x_prompt.shape[:2]

    def attend_prompt(jj, qb, qib, kiw, ki2, kb, vb):
        r3 = lambda a: a.reshape(bp, sp, a.shape[-1])
        tq = _row_tile(sp, 256)
        return _dsa_prompt(r3(qb), r3(qib), r3(kiw), r3(ki2), r3(kb), r3(vb), tq).reshape(bp * sp, A_Q)

    glr0_p = [jnp.zeros((bp, B_HEADS, B_KEY_DIM, B_VAL_DIM), state_glr.dtype) for _ in range(n_ab)]
    pool0_p = [jnp.zeros((bp, POOL_HALO, d), F32) for _ in range(n_c)]
    yp, op = _trunk(x_prompt, p_prompt, jnp.arange(sp, dtype=jnp.int32), attend_prompt, glr0_p,
                    _row_tile(sp, 256), pool0_p, 0, W)

    bd, ts = x_sample.shape[:2]
    past = page_table.shape[1] * PAGE_SIZE

    n_pool = cache_k.shape[1]
    cache_kt = jnp.transpose(cache_k, (0, 1, 3, 4, 2)).reshape(n_ab, n_pool, A_Q, PAGE_SIZE)
    cache_vt = jnp.transpose(cache_v, (0, 1, 3, 4, 2)).reshape(n_ab, n_pool, A_Q, PAGE_SIZE)
    cache_kit = jnp.transpose(cache_kidx, (0, 1, 3, 2))

    def attend_sample(jj, qb, qib, kiw, ki2, kb, vb):
        return _dsa_sample(qb, qib, kiw, ki2, kb, vb, cache_kt[jj], cache_vt[jj], cache_kit[jj], page_table, ts)

    pos_s = jnp.tile(past + jnp.arange(ts, dtype=jnp.int32), bd)
    pool0_s = [jnp.pad(state_pool[j].astype(F32), ((0, 0), (POOL_HALO - n_buf, 0), (0, 0))) for j in range(n_c)]
    ys, os_ = _trunk(x_sample, p_sample, pos_s, attend_sample, [state_glr[j] for j in range(n_ab)],
                     LANES, pool0_s, n_buf, W)

    def pool_rows(o, prev, t):
        res = []
        for j, (hist, t_pad) in enumerate(o["hist"]):
            if t >= n_buf:
                res.append(hist[:, POOL_HALO - (t_pad - t) - n_buf:POOL_HALO - (t_pad - t)])
            else:
                new = hist[:, POOL_HALO - t_pad:POOL_HALO - t_pad + t]
                res.append(jnp.concatenate([prev[j][:, t:].astype(new.dtype), new], axis=1))
        return jnp.stack(res)

    return (yp, ys, jnp.stack(op["k"]), jnp.stack(op["v"]), jnp.stack(op["ki"]), jnp.stack(op["s"]),
            pool_rows(op, None, sp).astype(x_prompt.dtype),
            jnp.stack(os_["k"]), jnp.stack(os_["v"]), jnp.stack(os_["ki"]), jnp.stack(os_["s"]),
            pool_rows(os_, state_pool, ts).astype(x_sample.dtype))
```

```python
import functools

import jax
import jax.numpy as jnp
from jax import lax
from jax.experimental import pallas as pl
from jax.experimental.pallas import tpu as pltpu

F32 = jnp.float32
BF16 = jnp.bfloat16

A_HEADS = 8
A_HEAD_DIM = 64
IDX_HEADS = 8
IDX_DIM = 64
TOPK_MAX = 256
ROPE_THETA = 10000.0
B_HEADS = 4
B_KEY_DIM = 128
B_VAL_DIM = 128
GLR_BLOCK = 16
POOL_WINDOWS = (2, 4, 8, 16)
POOL_HALO = 16
PAGE_SIZE = 128
RMS_EPS = 1e-6

A_Q = A_HEADS * A_HEAD_DIM
IDX_Q = IDX_HEADS * IDX_DIM
B_K = B_HEADS * B_KEY_DIM
B_V = B_HEADS * B_VAL_DIM

LANES = 128
VMEM_LIMIT = 56 * 1024 * 1024
INT_MIN = -(2 ** 31)
NEG_INF = float("-inf")


def _params(*sem):
    return pltpu.CompilerParams(dimension_semantics=sem, vmem_limit_bytes=VMEM_LIMIT)


def _rms(x, g):
    ms = jnp.mean(x * x, axis=-1, keepdims=True)
    return x * lax.rsqrt(ms + RMS_EPS) * g


def _dot(a, b):
    return jnp.dot(a, b, preferred_element_type=F32)


def _dot_nt(a, b):
    return lax.dot_general(a, b, (((1,), (1,)), ((), ())), preferred_element_type=F32)


def _const_spec(shape):
    nd = len(shape)
    return pl.BlockSpec(shape, lambda *_: (0,) * nd, pipeline_mode=pl.Buffered(1))


IN_GROUP_A = 4 * A_Q
IN_GROUP_I = LANES
IN_GROUP_B = 2 * B_K + 2 * B_V
IN_PACKED = IN_GROUP_A + IN_GROUP_I + IN_GROUP_B


def _in_proj_kernel(x_ref, g_ref, w_ref, cos_ref, sin_ref,
                    qb_ref, kt_ref, kb_ref, vt_ref, vb_ref, qib_ref, kiw_ref, kit_ref, ki2_ref, bg_ref):
    tm = x_ref.shape[0]
    h = _rms(x_ref[...], g_ref[...]).astype(BF16)
    cos = cos_ref[...]
    sin = sin_ref[...]
    lane = lax.broadcasted_iota(jnp.int32, (tm, LANES), 1)
    first = (lane % A_HEAD_DIM) < (A_HEAD_DIM // 2)
    slab = lambda i: slice(LANES * i, LANES * (i + 1))

    def rope(z, c, s):
        rot = jnp.where(first, pltpu.roll(z, LANES - A_HEAD_DIM // 2, 1), pltpu.roll(z, A_HEAD_DIM // 2, 1))
        return z * c + rot * s

    def roped(col0):
        z = _dot(h, w_ref[:, col0:col0 + A_Q])
        return [rope(z[:, slab(i)], cos, sin) for i in range(A_Q // LANES)]

    scale = A_HEAD_DIM ** -0.5
    for i, r in enumerate(roped(0)):
        qb_ref[:, slab(i)] = (r * scale).astype(BF16)
    for i, r in enumerate(roped(A_Q)):
        kt_ref[slab(i), :] = r.T
        kb_ref[:, slab(i)] = r.astype(BF16)
    v = _dot(h, w_ref[:, 2 * A_Q:3 * A_Q])
    for i in range(A_Q // LANES):
        vt_ref[slab(i), :] = v[:, slab(i)].T
    vb_ref[...] = v.astype(BF16)
    for i, r in enumerate(roped(3 * A_Q)):
        qib_ref[:, slab(i)] = (r * (IDX_DIM ** -0.5)).astype(BF16)
    is_key = lane < IDX_DIM
    z = _dot(h, w_ref[:, IN_GROUP_A:IN_GROUP_A + IN_GROUP_I])
    kiw = rope(z, jnp.where(is_key, cos, 1.0), jnp.where(is_key, sin, 0.0))
    kiw_ref[...] = kiw
    kit_ref[...] = kiw.T[:IDX_DIM, :]
    ki2_ref[...] = jnp.where(is_key, kiw, pltpu.roll(kiw, IDX_DIM, 1)).astype(BF16)
    col0 = IN_GROUP_A + IN_GROUP_I
    for i in range(IN_GROUP_B // A_Q):
        bg_ref[:, A_Q * i:A_Q * (i + 1)] = _dot(h, w_ref[:, col0 + A_Q * i:col0 + A_Q * (i + 1)])


def _in_proj(x, g, w_packed, cos_tab, sin_tab, tm):
    b, t, d = x.shape
    row = lambda width: pl.BlockSpec((None, tm, width), lambda bi, i: (bi, i, 0))
    col = lambda height: pl.BlockSpec((None, height, tm), lambda bi, i: (bi, 0, i))
    tab = pl.BlockSpec((tm, LANES), lambda bi, i: (i, 0))
    rows = lambda width, dt: (jax.ShapeDtypeStruct((b, t, width), dt), row(width))
    cols = lambda height: (jax.ShapeDtypeStruct((b, height, t), F32), col(height))
    outs = (
        rows(A_Q, BF16),
        cols(A_Q),
        rows(A_Q, BF16),
        cols(A_Q),
        rows(A_Q, BF16),
        rows(IDX_Q, BF16),
        rows(LANES, F32),
        cols(IDX_DIM),
        rows(LANES, BF16),
        rows(IN_GROUP_B, F32),
    )
    return pl.pallas_call(
        _in_proj_kernel,
        grid=(b, t // tm),
        in_specs=[row(d), _const_spec((1, d)), _const_spec((d, IN_PACKED)), tab, tab],
        out_specs=tuple(o[1] for o in outs),
        out_shape=tuple(o[0] for o in outs),
        compiler_params=_params("parallel", "parallel"),
    )(x, g, w_packed, cos_tab, sin_tab)


def _order_key(x):
    bits = pltpu.bitcast(x + 0.0, jnp.int32)
    return bits ^ ((bits >> 31) & jnp.int32(0x7FFFFFFF))


def _loop(n, body, init):
    if isinstance(n, int):
        for c in range(n):
            init = body(c, init)
        return init
    return lax.fori_loop(0, n, body, init)


def _fold_lanes(t):
    acc = t[:, :LANES]
    for i in range(1, t.shape[1] // LANES):
        acc = acc + t[:, LANES * i:LANES * (i + 1)]
    return acc


def _select_bias(n_c, width, get_key, get_bias, set_bias, level_ref, kk, use_topk):
    rows = use_topk.shape[0]
    kf = float(kk)
    zeros = jnp.zeros((rows, LANES), F32)

    def count(cmp, level):
        level_ref[...] = jnp.broadcast_to(level, (rows, LANES))

        def body(c, a):
            k = get_key(c)
            level_b = level_ref[...]
            for i in range(width // LANES):
                a = a + jnp.where(cmp(k[:, LANES * i:LANES * (i + 1)], level_b), 1.0, 0.0)
            return a

        return jnp.sum(_loop(n_c, body, zeros), axis=1, keepdims=True)

    def bisect(i, t):
        cand = t | jnp.left_shift(jnp.int32(1), 31 - i)
        return jnp.where(count(jnp.greater_equal, cand ^ jnp.int32(INT_MIN)) >= kf, cand, t)

    thr = lax.fori_loop(0, 32, bisect, jnp.zeros((rows, 1), jnp.int32)) ^ jnp.int32(INT_MIN)

    def apply(c, acc):
        ge = get_key(c) >= thr
        set_bias(c, jnp.where(use_topk, jnp.where(ge, 0.0, NEG_INF), get_bias(c)))
        return acc + _fold_lanes(jnp.where(ge, 1.0, 0.0))

    cnt_ge = jnp.sum(_loop(n_c, apply, zeros), axis=1, keepdims=True)
    tied = jnp.logical_and(use_topk, cnt_ge != kf)

    @pl.when(jnp.max(jnp.where(tied, 1.0, 0.0)) > 0.0)
    def _():
        room = kf - count(jnp.greater, thr)
        before = (lax.broadcasted_iota(jnp.int32, (width, width), 0)
                  < lax.broadcasted_iota(jnp.int32, (width, width), 1))
        tri = jnp.where(before, 1.0, 0.0).astype(BF16)

        def fix(c, seen):
            kc = get_key(c)
            eq = jnp.where(kc == thr, 1.0, 0.0)
            rank = _dot(eq.astype(BF16), tri) + seen
            keep = jnp.logical_or(kc > thr, jnp.logical_and(kc == thr, rank < room))
            set_bias(c, jnp.where(use_topk, jnp.where(keep, 0.0, NEG_INF), get_bias(c)))
            return seen + jnp.sum(eq, axis=1, keepdims=True)

        _loop(n_c, fix, jnp.zeros((rows, 1), F32))


def _pair_masks(rows):
    lane = lax.broadcasted_iota(jnp.int32, (rows, LANES), 1)
    lo = lane < A_HEAD_DIM
    return lo, jnp.logical_not(lo)


M_INIT = -1e30


def _dsa_prompt_kernel(kk, q_ref, qi_ref, kiw_ref, ki2_ref, k_ref, v_ref, o_ref,
                       key_ref, bias_ref, level_ref, qh_ref, s_ref, max_ref, sum_ref, acc_ref):
    tq = q_ref.shape[0]
    kc = key_ref.shape[2]
    j = pl.program_id(1)
    n_c = j + 1
    qpos = j * tq + lax.broadcasted_iota(jnp.int32, (tq, 1), 0)
    lane_k = lax.broadcasted_iota(jnp.int32, (1, kc), 1)
    masks = _pair_masks(tq)
    slab = lambda p: slice(LANES * p, LANES * (p + 1))
    keys_of = lambda c: pl.ds(pl.multiple_of(c * kc, kc), kc)
    one_head = lambda ref, hh: jnp.where(masks[hh % 2], ref[:, slab(hh // 2)], jnp.zeros((), ref.dtype))

    def init_bias(c, _):
        bias_ref[c] = jnp.where(c * kc + lane_k <= qpos, 0.0, NEG_INF)
        return 0

    lax.fori_loop(0, n_c, init_bias, 0)

    @pl.when(n_c * kc > kk)
    def _():
        w = kiw_ref[:, IDX_DIM:IDX_DIM + IDX_HEADS] * (IDX_HEADS ** -0.5)
        w_heads = [jnp.broadcast_to(w[:, hh:hh + 1], (tq, LANES)) for hh in range(IDX_HEADS)]
        for hh in range(IDX_HEADS):
            qh_ref[hh] = one_head(qi_ref, hh)

        def score_chunk(c, _):
            ki2 = ki2_ref[keys_of(c), :]
            acc = None
            for hh in range(IDX_HEADS):
                s = _dot_nt(qh_ref[hh], ki2)
                term = jnp.concatenate([jnp.maximum(s[:, slab(i)], 0.0) * w_heads[hh] for i in range(kc // LANES)],
                                       axis=1)
                acc = term if acc is None else acc + term
            key_ref[c] = _order_key(jnp.where(c * kc + lane_k <= qpos, acc, NEG_INF))
            return 0

        lax.fori_loop(0, n_c, score_chunk, 0)

        def set_bias(c, val):
            bias_ref[c] = val

        _select_bias(n_c, kc, lambda c: key_ref[c], lambda c: bias_ref[c], set_bias, level_ref, kk, qpos >= kk)

    tiles = range(kc // LANES)
    for hh in range(A_HEADS):
        qh_ref[hh] = one_head(q_ref, hh)
        max_ref[hh] = jnp.full((tq, LANES), M_INIT, F32)
        sum_ref[hh] = jnp.zeros((tq, LANES), F32)
    for p in range(A_Q // LANES):
        acc_ref[p] = jnp.zeros((tq, LANES), F32)

    def max_chunk(c, _):
        bias = bias_ref[c]
        for hh in range(A_HEADS):
            s = _dot_nt(qh_ref[hh], k_ref[keys_of(c), slab(hh // 2)]) + bias
            s_ref[hh, c] = s
            m = max_ref[hh]
            for i in tiles:
                m = jnp.maximum(m, s[:, slab(i)])
            max_ref[hh] = m
        return 0

    lax.fori_loop(0, n_c, max_chunk, 0)
    for hh in range(A_HEADS):
        max_ref[hh] = jnp.broadcast_to(jnp.max(max_ref[hh], axis=1, keepdims=True), (tq, LANES))

    def sum_chunk(c, _):
        for p in range(A_Q // LANES):
            pv = []
            for hh in (2 * p, 2 * p + 1):
                m = max_ref[hh]
                s = s_ref[hh, c]
                e = [jnp.exp(s[:, slab(i)] - m) for i in tiles]
                l = sum_ref[hh]
                for t in e:
                    l = l + t
                sum_ref[hh] = l
                pv.append(_dot(jnp.concatenate(e, axis=1).astype(BF16), v_ref[keys_of(c), slab(p)]))
            acc_ref[p] = acc_ref[p] + jnp.where(masks[0], pv[0], pv[1])
        return 0

    lax.fori_loop(0, n_c, sum_chunk, 0)
    for p in range(A_Q // LANES):
        l = [jnp.sum(sum_ref[hh], axis=1, keepdims=True) for hh in (2 * p, 2 * p + 1)]
        o_ref[:, slab(p)] = (acc_ref[p] / jnp.where(masks[0], l[0], l[1])).astype(o_ref.dtype)


def _dsa_prompt(qb, qib, kiw, ki2, kb, vb, tq):
    b, s, _ = qb.shape
    kk = min(TOPK_MAX, s // 4)
    tile = lambda width: pl.BlockSpec((None, tq, width), lambda bi, j: (bi, j, 0))
    full = lambda width: pl.BlockSpec((None, s, width), lambda bi, j: (bi, 0, 0))
    return pl.pallas_call(
        functools.partial(_dsa_prompt_kernel, kk),
        grid=(b, s // tq),
        in_specs=[tile(A_Q), tile(IDX_Q), tile(LANES), full(LANES), full(A_Q), full(A_Q)],
        out_specs=tile(A_Q),
        out_shape=jax.ShapeDtypeStruct((b, s, A_Q), BF16),
        scratch_shapes=[pltpu.VMEM((s // tq, tq, tq), jnp.int32), pltpu.VMEM((s // tq, tq, tq), F32),
                        pltpu.VMEM((tq, LANES), jnp.int32), pltpu.VMEM((A_HEADS, tq, LANES), BF16),
                        pltpu.VMEM((A_HEADS, s // tq, tq, tq), F32),
                        pltpu.VMEM((A_HEADS, tq, LANES), F32), pltpu.VMEM((A_HEADS, tq, LANES), F32),
                        pltpu.VMEM((A_Q // LANES, tq, LANES), F32)],
        compiler_params=_params("parallel", "arbitrary"),
    )(qb, qib, kiw, ki2, kb, vb)


TPAD = 8
NEW_PAD = LANES
CHUNK_PAGES = 16


def _dsa_sample_kernel(t_new, n_pages, kk, pt_ref, qbd_ref, qir_ref, wr_ref, knew_ref, vnew_ref, kinew_ref,
                       ckidx_ref, ck_ref, cv_ref, o_ref,
                       kibuf, kvbuf, logit_ref, key_ref, bias_ref, level_ref, sem_ki, sem_kv):
    b = pl.program_id(0)
    past = n_pages * PAGE_SIZE
    rows = A_HEADS * TPAD
    n_chunks = n_pages // CHUNK_PAGES

    def ki_copy(p):
        return pltpu.make_async_copy(ckidx_ref.at[pt_ref[b, p]], kibuf.at[p], sem_ki)

    def kv_copy(src_ref, c, i, slot):
        return pltpu.make_async_copy(src_ref.at[pt_ref[b, c * CHUNK_PAGES + i]], kvbuf.at[slot, i], sem_kv.at[slot])

    def start_chunk(src_ref, c, slot):
        lax.fori_loop(0, CHUNK_PAGES, lambda i, _: (kv_copy(src_ref, c, i, slot).start(), 0)[1], 0)

    def wait_chunk(src_ref, c, slot):
        lax.fori_loop(0, CHUNK_PAGES, lambda i, _: (kv_copy(src_ref, c, i, slot).wait(), 0)[1], 0)

    lax.fori_loop(0, n_pages, lambda p, _: (ki_copy(p).start(), 0)[1], 0)
    start_chunk(ck_ref, 0, 0)
    lax.fori_loop(0, n_pages, lambda p, _: (ki_copy(p).wait(), 0)[1], 0)

    qir = qir_ref[...]
    w = wr_ref[...] * (IDX_HEADS ** -0.5)
    tok = lax.broadcasted_iota(jnp.int32, (TPAD, 1), 0)
    tok_ok = tok < t_new

    def keys_of(s, ok):
        t = jnp.maximum(s, 0.0) * w
        return _order_key(jnp.where(ok, jnp.sum(t.reshape(IDX_HEADS, TPAD, LANES), axis=0), NEG_INF))

    def index_page(p, _):
        key_ref[p] = keys_of(_dot(qir, kibuf[p].astype(BF16)), tok_ok)
        return 0

    lax.fori_loop(0, n_pages, index_page, 0, unroll=4)
    new_idx = lax.broadcasted_iota(jnp.int32, (1, NEW_PAD), 1)
    new_ok = jnp.logical_and(jnp.logical_and(new_idx <= tok, new_idx < t_new), tok_ok)
    key_ref[n_pages] = keys_of(_dot_nt(qir, kinew_ref[...]), new_ok)
    bias_ref[0:n_pages] = jnp.broadcast_to(jnp.where(tok_ok, 0.0, NEG_INF)[None], (n_pages, TPAD, LANES))
    bias_ref[n_pages] = jnp.where(new_ok, 0.0, NEG_INF)

    def set_bias(c, val):
        bias_ref[c] = val

    use_topk = jnp.logical_and(past + tok + 1 > kk, tok_ok)
    _select_bias(n_pages + 1, LANES, lambda c: key_ref[c], lambda c: bias_ref[c], set_bias, level_ref, kk, use_topk)

    qbd = qbd_ref[...]
    for c in range(n_chunks):
        slot = c % 2
        wait_chunk(ck_ref, c, slot)
        if c + 1 < n_chunks:
            start_chunk(ck_ref, c + 1, 1 - slot)
        else:
            start_chunk(cv_ref, 0, 1 - slot)

        def logits_page(i, _, c=c, slot=slot):
            logit_ref[c * CHUNK_PAGES + i] = _dot(qbd, kvbuf[slot, i].astype(BF16))
            return 0

        lax.fori_loop(0, CHUNK_PAGES, logits_page, 0, unroll=2)
    logit_ref[n_pages] = _dot_nt(qbd, knew_ref[...])
    s = logit_ref[...].reshape(n_pages + 1, A_HEADS, TPAD, LANES) + bias_ref[...][:, None]
    s = jnp.where(tok_ok[None, None], s, 0.0).reshape(n_pages + 1, rows, LANES)
    m = jnp.max(jnp.max(s, axis=0), axis=1, keepdims=True)
    e = jnp.exp(s - m[None])
    l = jnp.sum(jnp.sum(e, axis=0), axis=1, keepdims=True)
    logit_ref[...] = e

    acc = _dot(logit_ref[n_pages].astype(BF16), vnew_ref[...])
    for c in range(n_chunks):
        slot = (n_chunks + c) % 2
        wait_chunk(cv_ref, c, slot)
        if c + 1 < n_chunks:
            start_chunk(cv_ref, c + 1, 1 - slot)

        def value_page(i, a, c=c, slot=slot):
            return a + _dot_nt(logit_ref[c * CHUNK_PAGES + i].astype(BF16), kvbuf[slot, i].astype(BF16))

        acc = lax.fori_loop(0, CHUNK_PAGES, value_page, acc, unroll=2)
    acc = (acc / l).reshape(A_HEADS, TPAD, A_Q)
    head_of_lane = lax.broadcasted_iota(jnp.int32, (TPAD, A_Q), 1) // A_HEAD_DIM
    out = jnp.zeros((TPAD, A_Q), F32)
    for hh in range(A_HEADS):
        out = out + jnp.where(head_of_lane == hh, acc[hh], 0.0)
    o_ref[...] = out.astype(o_ref.dtype)


def _dsa_sample(qb, qib, kiw, ki2, kb, vb, cache_kt, cache_vt, cache_kit, page_table, t_new):
    bd, n_pages = page_table.shape
    assert n_pages % CHUNK_PAGES == 0 and t_new <= TPAD
    past = n_pages * PAGE_SIZE
    kk = min(TOPK_MAX, (past + t_new) // 4)
    rows = A_HEADS * TPAD

    def pad_tokens(a, to):
        a = a.reshape(bd, t_new, a.shape[-1])
        return jnp.pad(a, ((0, 0), (0, to - t_new), (0, 0)))

    q8 = pad_tokens(qb, TPAD)
    head_of_lane = jnp.arange(A_Q) // A_HEAD_DIM
    qbd = jnp.where(head_of_lane[None, None, None, :] == jnp.arange(A_HEADS)[None, :, None, None],
                    q8[:, None], jnp.zeros((), BF16)).reshape(bd, rows, A_Q)
    qir = pad_tokens(qib, TPAD).reshape(bd, TPAD, IDX_HEADS, IDX_DIM).transpose(0, 2, 1, 3).reshape(bd, rows, IDX_DIM)
    wr = pad_tokens(kiw[:, IDX_DIM:IDX_DIM + IDX_HEADS], TPAD).transpose(0, 2, 1).reshape(bd, rows, 1)
    knew = pad_tokens(kb, NEW_PAD)
    vnew = pad_tokens(vb, NEW_PAD)
    kinew = pad_tokens(ki2[:, :IDX_DIM], NEW_PAD)

    per_seq = lambda r, c: pl.BlockSpec((None, r, c), lambda b, pt: (b, 0, 0))
    any_spec = pl.BlockSpec(memory_space=pl.ANY)
    grid_spec = pltpu.PrefetchScalarGridSpec(
        num_scalar_prefetch=1,
        grid=(bd,),
        in_specs=[per_seq(rows, A_Q), per_seq(rows, IDX_DIM), per_seq(rows, 1), per_seq(NEW_PAD, A_Q),
                  per_seq(NEW_PAD, A_Q), per_seq(NEW_PAD, IDX_DIM), any_spec, any_spec, any_spec],
        out_specs=per_seq(TPAD, A_Q),
        scratch_shapes=[
            pltpu.VMEM((n_pages, IDX_DIM, PAGE_SIZE), F32),
            pltpu.VMEM((2, CHUNK_PAGES, A_Q, PAGE_SIZE), F32),
            pltpu.VMEM((n_pages + 1, rows, LANES), F32),
            pltpu.VMEM((n_pages + 1, TPAD, LANES), jnp.int32),
            pltpu.VMEM((n_pages + 1, TPAD, LANES), F32),
            pltpu.VMEM((TPAD, LANES), jnp.int32),
            pltpu.SemaphoreType.DMA(()),
            pltpu.SemaphoreType.DMA((2,)),
        ],
    )
    out = pl.pallas_call(
        functools.partial(_dsa_sample_kernel, t_new, n_pages, kk),
        grid_spec=grid_spec,
        out_shape=jax.ShapeDtypeStruct((bd, TPAD, A_Q), BF16),
        compiler_params=_params("arbitrary"),
    )(page_table, qbd, qir, wr, knew, vnew, kinew, cache_kit, cache_kt, cache_vt)
    return out[:, :t_new].reshape(bd * t_new, A_Q)


def _glr_kernel(t_valid, layer, bg_ref, lbl_ref, gw_ref, s0_ref, o_ref, s_ref, st_ref):
    tt = bg_ref.shape[0]
    nb = tt // GLR_BLOCK
    it = pl.program_id(1)

    @pl.when(it == 0)
    def _():
        for hh in range(B_HEADS):
            st_ref[hh] = s0_ref[hh].astype(F32).T

    logits = lbl_ref[...]
    ex = jnp.exp(logits - jnp.max(logits, axis=0, keepdims=True))
    lb_all = jnp.sum(ex[:layer + 1], axis=0, keepdims=True) / jnp.sum(ex, axis=0, keepdims=True)

    row = lax.broadcasted_iota(jnp.int32, (tt, 1), 0)
    valid = (it * tt + row) < t_valid
    rloc = row % GLR_BLOCK
    t_idx = lax.broadcasted_iota(jnp.int32, (1, GLR_BLOCK, 1), 1)
    blk_of_col = lax.broadcasted_iota(jnp.int32, (nb, 1, tt), 2) // GLR_BLOCK
    blk_id = lax.broadcasted_iota(jnp.int32, (nb, 1, tt), 0)
    in_block = blk_of_col == blk_id
    gw = gw_ref[...]

    for hh in range(B_HEADS):
        lb = lb_all[:, B_KEY_DIM * hh:B_KEY_DIM * (hh + 1)]
        bq = bg_ref[:, B_KEY_DIM * hh:B_KEY_DIM * (hh + 1)]
        bf = bg_ref[:, B_K + B_KEY_DIM * hh:B_K + B_KEY_DIM * (hh + 1)]
        bi = bg_ref[:, 2 * B_K + B_VAL_DIM * hh:2 * B_K + B_VAL_DIM * (hh + 1)]
        bo = bg_ref[:, 2 * B_K + B_V + B_VAL_DIM * hh:2 * B_K + B_V + B_VAL_DIM * (hh + 1)]
        f = lb + (1.0 - lb) * jax.nn.sigmoid(bf)
        kx = jnp.where(valid, (1.0 - lb) * jax.nn.sigmoid(-bf), 0.0)
        qx = bq * jax.nn.sigmoid(bq)
        cum = jnp.where(valid, jnp.log(f), 0.0)
        for sh in (1, 2, 4, 8):
            cum = cum + jnp.where(rloc >= sh, pltpu.roll(cum, sh, 0), 0.0)
        b3 = cum.reshape(nb, GLR_BLOCK, B_KEY_DIM)
        q3 = qx.reshape(nb, GLR_BLOCK, B_KEY_DIM)
        k3 = kx.reshape(nb, GLR_BLOCK, B_KEY_DIM)
        v3 = bi.reshape(nb, GLR_BLOCK, B_VAL_DIM)
        b_end = b3[:, GLR_BLOCK - 1:GLR_BLOCK, :]
        q_dec = (q3 * jnp.exp(b3)).astype(BF16)
        k_dec = (k3 * jnp.exp(b_end - b3)).reshape(tt, B_KEY_DIM).astype(BF16)

        o_in = jnp.zeros((nb, GLR_BLOCK, B_VAL_DIM), F32)
        for s in range(GLR_BLOCK):
            dec = jnp.exp(jnp.where(t_idx >= s, b3 - b3[:, s:s + 1, :], NEG_INF))
            a = jnp.sum(q3 * k3[:, s:s + 1, :] * dec, axis=-1, keepdims=True)
            o_in = o_in + a * v3[:, s:s + 1, :]

        v_t = bi.T
        lhs = jnp.where(in_block, v_t[None], 0.0).astype(BF16).reshape(nb * B_VAL_DIM, tt)
        u_t = _dot(lhs, k_dec).reshape(nb, B_VAL_DIM, B_KEY_DIM)
        decay_end = jnp.exp(b_end)
        st = st_ref[hh]
        outs = []
        for jb in range(nb):
            outs.append(_dot_nt(q_dec[jb], st.astype(BF16)))
            st = st * decay_end[jb] + u_t[jb]
        st_ref[hh] = st
        o = jnp.concatenate(outs, axis=0) + o_in.reshape(tt, B_VAL_DIM)
        o = _rms(o, gw) * (bo * jax.nn.sigmoid(bo))
        o_ref[:, B_VAL_DIM * hh:B_VAL_DIM * (hh + 1)] = o.astype(o_ref.dtype)

    @pl.when(it == pl.num_programs(1) - 1)
    def _():
        for hh in range(B_HEADS):
            s_ref[hh] = st_ref[hh].T.astype(s_ref.dtype)


def _glr(bg4, lb_logits, glr_norm_w, s0, t_valid, tt, layer):
    b, t, _ = bg4.shape
    state = pl.BlockSpec((None, B_HEADS, B_KEY_DIM, B_VAL_DIM), lambda bi, i: (bi, 0, 0, 0))
    return pl.pallas_call(
        functools.partial(_glr_kernel, t_valid, layer),
        grid=(b, t // tt),
        in_specs=[pl.BlockSpec((None, tt, IN_GROUP_B), lambda bi, i: (bi, i, 0)),
                  pl.BlockSpec(lb_logits.shape, lambda bi, i: (0, 0)),
                  pl.BlockSpec((1, B_VAL_DIM), lambda bi, i: (0, 0)),
                  state],
        out_specs=(pl.BlockSpec((None, tt, B_V), lambda bi, i: (bi, i, 0)), state),
        out_shape=(jax.ShapeDtypeStruct((b, t, B_V), BF16), jax.ShapeDtypeStruct(s0.shape, s0.dtype)),
        scratch_shapes=[pltpu.VMEM((B_HEADS, B_VAL_DIM, B_KEY_DIM), F32)],
        compiler_params=_params("parallel", "arbitrary"),
    )(bg4, lb_logits, glr_norm_w, s0)


def _pool_kernel(n_hist, x_ref, prev_ref, g_ref, wp_ref, sc_ref, o_ref, hist_ref, carry_ref):
    tt, d = x_ref.shape
    group = d // len(POOL_WINDOWS)
    it = pl.program_id(1)

    @pl.when(it == 0)
    def _():
        carry_ref[...] = prev_ref[...]

    x = x_ref[...]
    h = _rms(x, g_ref[...])
    ext = jnp.concatenate([carry_ref[...], h], axis=0)
    pos = it * tt + lax.broadcasted_iota(jnp.int32, (tt, 1), 0) + n_hist
    run = ext
    width = 1
    for gi, win in enumerate(POOL_WINDOWS):
        while width < win:
            run = run + pltpu.roll(run, width, 0)
            width *= 2
        sl = slice(group * gi, group * (gi + 1))
        cnt = jnp.minimum(pos + 1, win).astype(F32)
        z = run[POOL_HALO:, sl] / cnt - h[:, sl]
        y = _dot(z.astype(BF16), wp_ref[gi]) * sc_ref[:, sl]
        o_ref[:, sl] = x[:, sl] + y
    tail = ext[tt:, :]
    carry_ref[...] = tail
    hist_ref[...] = tail


def _pool(x, prev, g, w_pool, scale, n_hist, tt):
    b, t, d = x.shape
    hist_spec = pl.BlockSpec((None, POOL_HALO, d), lambda bi, i: (bi, 0, 0))
    return pl.pallas_call(
        functools.partial(_pool_kernel, n_hist),
        grid=(b, t // tt),
        in_specs=[pl.BlockSpec((None, tt, d), lambda bi, i: (bi, i, 0)), hist_spec,
                  pl.BlockSpec((1, d), lambda bi, i: (0, 0)),
                  pl.BlockSpec(w_pool.shape, lambda bi, i: (0, 0, 0)),
                  pl.BlockSpec((1, d), lambda bi, i: (0, 0))],
        out_specs=(pl.BlockSpec((None, tt, d), lambda bi, i: (bi, i, 0)), hist_spec),
        out_shape=(jax.ShapeDtypeStruct((b, t, d), F32), jax.ShapeDtypeStruct((b, POOL_HALO, d), F32)),
        scratch_shapes=[pltpu.VMEM((POOL_HALO, d), F32)],
        compiler_params=_params("parallel", "arbitrary"),
    )(x, prev, g, w_pool, scale)


def _ffn_ple_kernel(n_chunks, has_mix, final, *refs):
    if has_mix:
        att_ref, glr_ref, wa_ref, wb_ref = refs[:4]
        refs = refs[4:]
    x_ref, p_ref, gf_ref, wg_ref, wu_ref, wd_ref, gp_ref, wpg_ref, wpe_ref, gfin_ref, o_ref = refs
    x = x_ref[...]
    if has_mix:
        x = x + _dot(att_ref[...], wa_ref[...]) + _dot(glr_ref[...], wb_ref[...])
    h = _rms(x, gf_ref[...]).astype(BF16)
    d_ff = wg_ref.shape[1]
    cw = d_ff // n_chunks
    acc = None
    for c in range(n_chunks):
        sl = slice(cw * c, cw * (c + 1))
        gate = _dot(h, wg_ref[:, sl])
        up = _dot(h, wu_ref[:, sl])
        part = _dot((gate * jax.nn.sigmoid(gate) * up).astype(BF16), wd_ref[sl, :])
        acc = part if acc is None else acc + part
    x = x + acc
    gate = jax.nn.sigmoid(_dot(_rms(x, gp_ref[...]).astype(BF16), wpg_ref[...]))
    x = x + _dot(p_ref[...].astype(BF16), wpe_ref[...]) * gate
    if final:
        x = _rms(x, gfin_ref[...])
    o_ref[...] = x


def _ffn_ple(x, mix, p, g_ffn, w_gate, w_up, w_down, g_ple, w_ple_gate, w_ple, g_final, final, tm):
    n, d = x.shape
    d_ff = w_gate.shape[1]
    n_chunks = 2 if d_ff % (2 * LANES) == 0 else 1
    row = lambda width: pl.BlockSpec((tm, width), lambda i: (i, 0))
    args = [x, p, g_ffn, w_gate, w_up, w_down, g_ple, w_ple_gate, w_ple, g_final]
    specs = [row(d), row(p.shape[1])] + [_const_spec(a.shape) for a in args[2:]]
    if mix is not None:
        args = list(mix) + args
        specs = [row(mix[0].shape[1]), row(mix[1].shape[1]), _const_spec(mix[2].shape), _const_spec(mix[3].shape)] + specs
    return pl.pallas_call(
        functools.partial(_ffn_ple_kernel, n_chunks, mix is not None, final),
        grid=(n // tm,),
        in_specs=specs,
        out_specs=row(d),
        out_shape=jax.ShapeDtypeStruct((n, d), F32),
        compiler_params=_params("parallel"),
    )(*args)


def _rope_tables(pos):
    half = A_HEAD_DIM // 2
    inv = ROPE_THETA ** (-jnp.arange(half, dtype=F32) / half)
    ang = pos.astype(F32)[:, None] * inv[None, :]
    c, s = jnp.cos(ang), jnp.sin(ang)
    reps = LANES // A_HEAD_DIM
    return jnp.tile(jnp.concatenate([c, c], axis=1), (1, reps)), jnp.tile(jnp.concatenate([-s, s], axis=1), (1, reps))


def _pack_w_in(w):
    d = w.shape[0]
    n_a = 3 * A_Q + IDX_Q
    n_i = IDX_DIM + IDX_HEADS
    return jnp.concatenate([w[:, :n_a], w[:, n_a:n_a + n_i], jnp.zeros((d, IN_GROUP_I - n_i), w.dtype),
                            w[:, n_a + n_i:]], axis=1).astype(BF16)


def _row_tile(n, want):
    t = min(n, want)
    assert n % t == 0
    return t


def _trunk(x, p, pos, attend, glr_s0, glr_tile, pool_prev, pool_hist, W):
    b, t, d = x.shape
    n = b * t
    tm = _row_tile(n, 256)
    row2 = lambda a: a.reshape(1, -1)
    cos_tab, sin_tab = _rope_tables(pos)
    depth = W["mix_norm"].shape[0]
    xf = x.reshape(n, d)
    outs = {}
    for i in range(depth):
        jj = i // 2
        if i % 2 == 0:
            bs, ts = (b, t) if t % tm == 0 else (1, n)
            res = _in_proj(xf.reshape(bs, ts, d), row2(W["mix_norm"][i]), _pack_w_in(W["w_in"][jj]),
                           cos_tab, sin_tab, tm)
            flat = lambda a: a.reshape(n, a.shape[2])
            qb, kt, kb, vt, vb, qib, kiw, kit, ki2, bg4 = res
            qb, kb, vb, qib, kiw, ki2, bg4 = [flat(a) for a in (qb, kb, vb, qib, kiw, ki2, bg4)]
            att = attend(jj, qb, qib, kiw, ki2, kb, vb)
            t_pad = -(-t // glr_tile) * glr_tile
            bg3 = bg4.reshape(b, t, IN_GROUP_B)
            if t_pad != t:
                bg3 = jnp.pad(bg3, ((0, 0), (0, t_pad - t), (0, 0)))
            glr, s_new = _glr(bg3, W["lb_logits"], row2(W["glr_norm"][jj]), glr_s0[jj], t, glr_tile, jj)
            glr = glr[:, :t].reshape(n, B_V)
            w_out = W["w_out"][jj].astype(BF16)
            mix = (att, glr, w_out[:A_Q], w_out[A_Q:])
            heads = lambda a: a.reshape(bs, A_HEADS, A_HEAD_DIM, ts).transpose(0, 3, 1, 2).reshape(
                b, t, A_HEADS, A_HEAD_DIM)
            outs.setdefault("k", []).append(heads(kt))
            outs.setdefault("v", []).append(heads(vt))
            outs.setdefault("ki", []).append(kit.transpose(0, 2, 1).reshape(b, t, IDX_DIM))
            outs.setdefault("s", []).append(s_new)
        else:
            tt = _row_tile(t, 256)
            t_pad = -(-t // 8) * 8
            x3 = xf.reshape(b, t, d)
            if t_pad != t:
                x3 = jnp.pad(x3, ((0, 0), (0, t_pad - t), (0, 0)))
                tt = t_pad
            y3, hist = _pool(x3, pool_prev[jj], row2(W["mix_norm"][i]), W["pool_w"][jj].astype(BF16),
                             row2(W["pool_scale"][jj]), pool_hist, tt)
            xf = y3[:, :t].reshape(n, d)
            mix = None
            outs.setdefault("hist", []).append((hist, t_pad))
        d_ff = W["w_down"].shape[1]
        w_up = W["w_up"][i].astype(BF16)
        xf = _ffn_ple(xf, mix, p[i].reshape(n, -1), row2(W["ffn_norm"][i]), w_up[:, :d_ff], w_up[:, d_ff:],
                      W["w_down"][i].astype(BF16), row2(W["ple_norm"][i]), W["w_ple_gate"][i].astype(BF16),
                      W["w_ple"][i].astype(BF16), row2(W["final_norm"]), i == depth - 1, tm)
    return xf.reshape(b, t, d), outs


def kernel(x_prompt, x_sample, cache_k, cache_v, cache_kidx, state_glr, state_pool, page_table, p_prompt, p_sample,
           mix_norm, w_in, w_out, lb_logits, glr_norm, pool_w, pool_scale, ffn_norm, w_up, w_down, ple_norm,
           w_ple_gate, w_ple, final_norm):
    W = dict(mix_norm=mix_norm, w_in=w_in, w_out=w_out, lb_logits=lb_logits, glr_norm=glr_norm, pool_w=pool_w,
             pool_scale=pool_scale, ffn_norm=ffn_norm, w_up=w_up, w_down=w_down, ple_norm=ple_norm,
             w_ple_gate=w_ple_gate, w_ple=w_ple, final_norm=final_norm)
    n_ab = w_in.shape[0]
    n_c = pool_w.shape[0]
    d = x_prompt.shape[-1]
    n_buf = max(POOL_WINDOWS) - 1

    bp, sp = x_prompt.shape[:2]

    def attend_prompt(jj, qb, qib, kiw, ki2, kb, vb):
        r3 = lambda a: a.reshape(bp, sp, a.shape[-1])
        tq = _row_tile(sp, 256)
        return _dsa_prompt(r3(qb), r3(qib), r3(kiw), r3(ki2), r3(kb), r3(vb), tq).reshape(bp * sp, A_Q)

    glr0_p = [jnp.zeros((bp, B_HEADS, B_KEY_DIM, B_VAL_DIM), state_glr.dtype) for _ in range(n_ab)]
    pool0_p = [jnp.zeros((bp, POOL_HALO, d), F32) for _ in range(n_c)]
    yp, op = _trunk(x_prompt, p_prompt, jnp.arange(sp, dtype=jnp.int32), attend_prompt, glr0_p,
                    _row_tile(sp, 256), pool0_p, 0, W)

    bd, ts = x_sample.shape[:2]
    past = page_table.shape[1] * PAGE_SIZE

    n_pool = cache_k.shape[1]
    cache_kt = jnp.transpose(cache_k, (0, 1, 3, 4, 2)).reshape(n_ab, n_pool, A_Q, PAGE_SIZE)
    cache_vt = jnp.transpose(cache_v, (0, 1, 3, 4, 2)).reshape(n_ab, n_pool, A_Q, PAGE_SIZE)
    cache_kit = jnp.transpose(cache_kidx, (0, 1, 3, 2))

    def attend_sample(jj, qb, qib, kiw, ki2, kb, vb):
        return _dsa_sample(qb, qib, kiw, ki2, kb, vb, cache_kt[jj], cache_vt[jj], cache_kit[jj], page_table, ts)

    pos_s = jnp.tile(past + jnp.arange(ts, dtype=jnp.int32), bd)
    pool0_s = [jnp.pad(state_pool[j].astype(F32), ((0, 0), (POOL_HALO - n_buf, 0), (0, 0))) for j in range(n_c)]
    ys, os_ = _trunk(x_sample, p_sample, pos_s, attend_sample, [state_glr[j] for j in range(n_ab)],
                     LANES, pool0_s, n_buf, W)

    def pool_rows(o, prev, t):
        res = []
        for j, (hist, t_pad) in enumerate(o["hist"]):
            if t >= n_buf:
                res.append(hist[:, POOL_HALO - (t_pad - t) - n_buf:POOL_HALO - (t_pad - t)])
            else:
                new = hist[:, POOL_HALO - t_pad:POOL_HALO - t_pad + t]
                res.append(jnp.concatenate([prev[j][:, t:].astype(new.dtype), new], axis=1))
        return jnp.stack(res)

    return (yp, ys, jnp.stack(op["k"]), jnp.stack(op["v"]), jnp.stack(op["ki"]), jnp.stack(op["s"]),
            pool_rows(op, None, sp).astype(x_prompt.dtype),
            jnp.stack(os_["k"]), jnp.stack(os_["v"]), jnp.stack(os_["ki"]), jnp.stack(os_["s"]),
            pool_rows(os_, state_pool, ts).astype(x_sample.dtype))
```

```python
import functools

import jax
import jax.numpy as jnp
from jax import lax
from jax.experimental import pallas as pl
from jax.experimental.pallas import tpu as pltpu

F32 = jnp.float32
BF16 = jnp.bfloat16

A_HEADS = 8
A_HEAD_DIM = 64
IDX_HEADS = 8
IDX_DIM = 64
TOPK_MAX = 256
ROPE_THETA = 10000.0
B_HEADS = 4
B_KEY_DIM = 128
B_VAL_DIM = 128
GLR_BLOCK = 16
POOL_WINDOWS = (2, 4, 8, 16)
POOL_HALO = 16
PAGE_SIZE = 128
RMS_EPS = 1e-6

A_Q = A_HEADS * A_HEAD_DIM
IDX_Q = IDX_HEADS * IDX_DIM
B_K = B_HEADS * B_KEY_DIM
B_V = B_HEADS * B_VAL_DIM

LANES = 128
VMEM_LIMIT = 56 * 1024 * 1024
INT_MIN = -(2 ** 31)
NEG_INF = float("-inf")


def _params(*sem):
    return pltpu.CompilerParams(dimension_semantics=sem, vmem_limit_bytes=VMEM_LIMIT)


def _rms(x, g):
    ms = jnp.mean(x * x, axis=-1, keepdims=True)
    return x * lax.rsqrt(ms + RMS_EPS) * g


def _dot(a, b):
    return jnp.dot(a, b, preferred_element_type=F32)


def _dot_nt(a, b):
    return lax.dot_general(a, b, (((1,), (1,)), ((), ())), preferred_element_type=F32)


def _const_spec(shape):
    nd = len(shape)
    return pl.BlockSpec(shape, lambda *_: (0,) * nd, pipeline_mode=pl.Buffered(1))


IN_GROUP_A = 4 * A_Q
IN_GROUP_I = LANES
IN_GROUP_B = 2 * B_K + 2 * B_V
IN_PACKED = IN_GROUP_A + IN_GROUP_I + IN_GROUP_B


def _in_proj_kernel(x_ref, g_ref, w_ref, cos_ref, sin_ref,
                    qb_ref, kt_ref, kb_ref, vt_ref, vb_ref, vtb_ref, qib_ref, kiw_ref, kit_ref, wt_ref, ki2_ref,
                    bg_ref):
    tm = x_ref.shape[0]
    h = _rms(x_ref[...], g_ref[...]).astype(BF16)
    cos = cos_ref[...]
    sin = sin_ref[...]
    lane = lax.broadcasted_iota(jnp.int32, (tm, LANES), 1)
    first = (lane % A_HEAD_DIM) < (A_HEAD_DIM // 2)
    slab = lambda i: slice(LANES * i, LANES * (i + 1))

    def rope(z, c, s):
        rot = jnp.where(first, pltpu.roll(z, LANES - A_HEAD_DIM // 2, 1), pltpu.roll(z, A_HEAD_DIM // 2, 1))
        return z * c + rot * s

    def roped(col0):
        z = _dot(h, w_ref[:, col0:col0 + A_Q])
        return [rope(z[:, slab(i)], cos, sin) for i in range(A_Q // LANES)]

    scale = A_HEAD_DIM ** -0.5
    for i, r in enumerate(roped(0)):
        qb_ref[:, slab(i)] = (r * scale).astype(BF16)
    for i, r in enumerate(roped(A_Q)):
        kt_ref[slab(i), :] = r.T
        kb_ref[:, slab(i)] = r.astype(BF16)
    v = _dot(h, w_ref[:, 2 * A_Q:3 * A_Q])
    for i in range(A_Q // LANES):
        v_t = v[:, slab(i)].T
        vt_ref[slab(i), :] = v_t
        vtb_ref[slab(i), :] = v_t.astype(BF16)
    vb_ref[...] = v.astype(BF16)
    for i, r in enumerate(roped(3 * A_Q)):
        qib_ref[:, slab(i)] = (r * (IDX_DIM ** -0.5)).astype(BF16)
    is_key = lane < IDX_DIM
    z = _dot(h, w_ref[:, IN_GROUP_A:IN_GROUP_A + IN_GROUP_I])
    kiw = rope(z, jnp.where(is_key, cos, 1.0), jnp.where(is_key, sin, 0.0))
    kiw_ref[...] = kiw
    kiw_t = kiw.T
    kit_ref[...] = kiw_t[:IDX_DIM, :]
    wt_ref[...] = kiw_t[IDX_DIM:IDX_DIM + IDX_HEADS, :]
    ki2_ref[...] = jnp.where(is_key, kiw, pltpu.roll(kiw, IDX_DIM, 1)).astype(BF16)
    col0 = IN_GROUP_A + IN_GROUP_I
    for i in range(IN_GROUP_B // A_Q):
        bg_ref[:, A_Q * i:A_Q * (i + 1)] = _dot(h, w_ref[:, col0 + A_Q * i:col0 + A_Q * (i + 1)])


def _in_proj(x, g, w_packed, cos_tab, sin_tab, tm):
    b, t, d = x.shape
    row = lambda width: pl.BlockSpec((None, tm, width), lambda bi, i: (bi, i, 0))
    col = lambda height: pl.BlockSpec((None, height, tm), lambda bi, i: (bi, 0, i))
    tab = pl.BlockSpec((tm, LANES), lambda bi, i: (i, 0))
    rows = lambda width, dt: (jax.ShapeDtypeStruct((b, t, width), dt), row(width))
    cols = lambda height: (jax.ShapeDtypeStruct((b, height, t), F32), col(height))
    outs = (
        rows(A_Q, BF16),
        cols(A_Q),
        rows(A_Q, BF16),
        cols(A_Q),
        rows(A_Q, BF16),
        (jax.ShapeDtypeStruct((b, t // tm, A_Q, tm), BF16),
         pl.BlockSpec((None, None, A_Q, tm), lambda bi, i: (bi, i, 0, 0))),
        rows(IDX_Q, BF16),
        rows(LANES, F32),
        cols(IDX_DIM),
        cols(IDX_HEADS),
        rows(LANES, BF16),
        rows(IN_GROUP_B, F32),
    )
    return pl.pallas_call(
        _in_proj_kernel,
        grid=(b, t // tm),
        in_specs=[row(d), _const_spec((1, d)), _const_spec((d, IN_PACKED)), tab, tab],
        out_specs=tuple(o[1] for o in outs),
        out_shape=tuple(o[0] for o in outs),
        compiler_params=_params("parallel", "parallel"),
    )(x, g, w_packed, cos_tab, sin_tab)


def _order_key(x):
    bits = pltpu.bitcast(x + 0.0, jnp.int32)
    return bits ^ ((bits >> 31) & jnp.int32(0x7FFFFFFF))


def _unrolled(n, body, init):
    for c in range(n):
        init = body(c, init)
    return init


def _fold_lanes(t):
    acc = t[:, :LANES]
    for i in range(1, t.shape[1] // LANES):
        acc = acc + t[:, LANES * i:LANES * (i + 1)]
    return acc


def _select_bias(n_c, width, get_key, get_bias, set_bias, kk, use_topk):
    rows = use_topk.shape[0]
    kf = float(kk)
    zeros = jnp.zeros((rows, LANES), F32)

    def count(cmp, level):
        level_b = jnp.broadcast_to(level, (rows, LANES))
        parts = [zeros] * 4
        for c in range(n_c):
            k = get_key(c)
            for i in range(width // LANES):
                hit = jnp.where(cmp(k[:, LANES * i:LANES * (i + 1)], level_b), 1.0, 0.0)
                parts[(c + i) % 4] = parts[(c + i) % 4] + hit
        return jnp.sum((parts[0] + parts[1]) + (parts[2] + parts[3]), axis=1, keepdims=True)

    def bisect(i, t):
        cand = t | jnp.left_shift(jnp.int32(1), 31 - i)
        return jnp.where(count(jnp.greater_equal, cand ^ jnp.int32(INT_MIN)) >= kf, cand, t)

    thr = lax.fori_loop(0, 32, bisect, jnp.zeros((rows, 1), jnp.int32)) ^ jnp.int32(INT_MIN)

    def apply(c, acc):
        ge = get_key(c) >= thr
        set_bias(c, jnp.where(use_topk, jnp.where(ge, 0.0, NEG_INF), get_bias(c)))
        return acc + _fold_lanes(jnp.where(ge, 1.0, 0.0))

    cnt_ge = jnp.sum(_unrolled(n_c, apply, zeros), axis=1, keepdims=True)
    tied = jnp.logical_and(use_topk, cnt_ge != kf)

    @pl.when(jnp.max(jnp.where(tied, 1.0, 0.0)) > 0.0)
    def _():
        room = kf - count(jnp.greater, thr)
        before = (lax.broadcasted_iota(jnp.int32, (width, width), 0)
                  < lax.broadcasted_iota(jnp.int32, (width, width), 1))
        tri = jnp.where(before, 1.0, 0.0).astype(BF16)

        def fix(c, seen):
            kc = get_key(c)
            eq = jnp.where(kc == thr, 1.0, 0.0)
            rank = _dot(eq.astype(BF16), tri) + seen
            keep = jnp.logical_or(kc > thr, jnp.logical_and(kc == thr, rank < room))
            set_bias(c, jnp.where(use_topk, jnp.where(keep, 0.0, NEG_INF), get_bias(c)))
            return seen + jnp.sum(eq, axis=1, keepdims=True)

        _unrolled(n_c, fix, jnp.zeros((rows, 1), F32))


def _pair_masks(rows):
    lane = lax.broadcasted_iota(jnp.int32, (rows, LANES), 1)
    lo = lane < A_HEAD_DIM
    return lo, jnp.logical_not(lo)


M_INIT = -1e30


SUBLANES = 8
LOG2_E = 1.4426950408889634


def _fold_rows(a, op):
    return op(a.reshape(a.shape[0] // SUBLANES, SUBLANES, a.shape[1]), axis=0)


def _select_bias_t(n_c, key_ref, bias_ref, kk, use_topk):
    kc, rows = key_ref.shape[1:]
    kf = float(kk)
    zeros = jnp.zeros((SUBLANES, rows), F32)

    def count(cmp, level):
        level8 = jnp.broadcast_to(level, (SUBLANES, rows))[None, None]

        def body(c, a):
            k4 = key_ref[c].reshape(4, kc // (4 * SUBLANES), SUBLANES, rows)
            part = jnp.sum(jnp.where(cmp(k4, level8), 1.0, 0.0), axis=1)
            return a + ((part[0] + part[1]) + (part[2] + part[3]))

        return jnp.sum(lax.fori_loop(0, n_c, body, zeros), axis=0, keepdims=True)

    def bisect(i, t):
        cand = t | jnp.left_shift(jnp.int32(1), 31 - i)
        return jnp.where(count(jnp.greater_equal, cand ^ jnp.int32(INT_MIN)) >= kf, cand, t)

    thr = lax.fori_loop(0, 32, bisect, jnp.zeros((1, rows), jnp.int32)) ^ jnp.int32(INT_MIN)

    def apply(c, acc):
        ge = key_ref[c] >= thr
        bias_ref[c] = jnp.where(use_topk, jnp.where(ge, 0.0, NEG_INF), bias_ref[c])
        return acc + _fold_rows(jnp.where(ge, 1.0, 0.0), jnp.sum)

    cnt_ge = jnp.sum(lax.fori_loop(0, n_c, apply, zeros), axis=0, keepdims=True)
    tied = jnp.logical_and(use_topk, cnt_ge != kf)

    @pl.when(jnp.max(jnp.where(tied, 1.0, 0.0)) > 0.0)
    def _():
        room = kf - count(jnp.greater, thr)
        earlier = (lax.broadcasted_iota(jnp.int32, (kc, kc), 1) < lax.broadcasted_iota(jnp.int32, (kc, kc), 0))
        tri = jnp.where(earlier, 1.0, 0.0).astype(BF16)

        def fix(c, seen):
            k = key_ref[c]
            eq = jnp.where(k == thr, 1.0, 0.0)
            rank = _dot(tri, eq.astype(BF16)) + seen
            keep = jnp.logical_or(k > thr, jnp.logical_and(k == thr, rank < room))
            bias_ref[c] = jnp.where(use_topk, jnp.where(keep, 0.0, NEG_INF), bias_ref[c])
            return seen + jnp.sum(eq, axis=0, keepdims=True)

        lax.fori_loop(0, n_c, fix, jnp.zeros((1, rows), F32))


def _dsa_prompt_kernel(kk, q_ref, qi_ref, wt_ref, ki2_ref, k_ref, vt_ref, o_ref,
                       key_ref, bias_ref, qh_ref, s_ref, max_ref, sum_ref, acc_ref):
    tq = q_ref.shape[0]
    kc = key_ref.shape[1]
    j = pl.program_id(1)
    n_c = j + 1
    qpos = j * tq + lax.broadcasted_iota(jnp.int32, (1, tq), 1)
    krow = lax.broadcasted_iota(jnp.int32, (kc, 1), 0)
    masks = _pair_masks(tq)
    slab = lambda p: slice(LANES * p, LANES * (p + 1))
    head_rows = lambda hh: slice(A_HEAD_DIM * hh, A_HEAD_DIM * (hh + 1))
    keys_of = lambda c: pl.ds(pl.multiple_of(c * kc, kc), kc)
    one_head = lambda ref, hh: jnp.where(masks[hh % 2], ref[:, slab(hh // 2)], jnp.zeros((), ref.dtype))
    causal = lambda c: c * kc + krow <= qpos

    def init_bias(c, _):
        bias_ref[c] = jnp.where(causal(c), 0.0, NEG_INF)
        return 0

    lax.fori_loop(0, n_c, init_bias, 0)

    @pl.when(n_c * kc > kk)
    def _():
        w = wt_ref[...] * (IDX_HEADS ** -0.5)
        for hh in range(IDX_HEADS):
            qh_ref[hh] = one_head(qi_ref, hh)

        def score_chunk(c, _):
            ki2 = ki2_ref[keys_of(c), :]
            acc = None
            for hh in range(IDX_HEADS):
                term = jnp.maximum(_dot_nt(ki2, qh_ref[hh]), 0.0) * w[hh:hh + 1, :]
                acc = term if acc is None else acc + term
            key_ref[c] = _order_key(jnp.where(causal(c), acc, NEG_INF))
            return 0

        lax.fori_loop(0, n_c, score_chunk, 0)
        _select_bias_t(n_c, key_ref, bias_ref, kk, qpos >= kk)

    for hh in range(A_HEADS):
        qh_ref[hh] = one_head(q_ref, hh)
        max_ref[hh] = jnp.full((SUBLANES, tq), M_INIT, F32)
        sum_ref[hh] = jnp.zeros((SUBLANES, tq), F32)
    acc_ref[...] = jnp.zeros(acc_ref.shape, F32)

    def max_chunk(c, _):
        bias = bias_ref[c]
        for hh in range(A_HEADS):
            s = _dot_nt(k_ref[keys_of(c), slab(hh // 2)], qh_ref[hh]) * LOG2_E + bias
            s_ref[hh, c] = s
            max_ref[hh] = jnp.maximum(max_ref[hh], _fold_rows(s, jnp.max))
        return 0

    lax.fori_loop(0, n_c, max_chunk, 0)
    for hh in range(A_HEADS):
        max_ref[hh] = jnp.broadcast_to(jnp.max(max_ref[hh], axis=0, keepdims=True), (SUBLANES, tq))

    def sum_chunk(c, _):
        for hh in range(A_HEADS):
            e = jnp.exp2(s_ref[hh, c].reshape(kc // SUBLANES, SUBLANES, tq) - max_ref[hh][None])
            sum_ref[hh] = sum_ref[hh] + jnp.sum(e, axis=0)
            pv = _dot(vt_ref[c, head_rows(hh), :], e.reshape(kc, tq).astype(BF16))
            acc_ref[head_rows(hh), :] = acc_ref[head_rows(hh), :] + pv
        return 0

    lax.fori_loop(0, n_c, sum_chunk, 0)
    for p in range(A_Q // LANES):
        parts = [acc_ref[head_rows(hh), :] / jnp.sum(sum_ref[hh], axis=0, keepdims=True) for hh in (2 * p, 2 * p + 1)]
        o_ref[:, slab(p)] = jnp.concatenate(parts, axis=0).T.astype(o_ref.dtype)


def _dsa_prompt(qb, qib, wt, ki2, kb, vtb, tq):
    b, s, _ = qb.shape
    n_c = s // tq
    assert vtb.shape == (b, n_c, A_Q, tq)
    kk = min(TOPK_MAX, s // 4)
    tile = lambda width: pl.BlockSpec((None, tq, width), lambda bi, j: (bi, j, 0))
    full = lambda width: pl.BlockSpec((None, s, width), lambda bi, j: (bi, 0, 0))
    return pl.pallas_call(
        functools.partial(_dsa_prompt_kernel, kk),
        grid=(b, n_c),
        in_specs=[tile(A_Q), tile(IDX_Q), pl.BlockSpec((None, IDX_HEADS, tq), lambda bi, j: (bi, 0, j)),
                  full(LANES), full(A_Q), pl.BlockSpec((None, n_c, A_Q, tq), lambda bi, j: (bi, 0, 0, 0))],
        out_specs=tile(A_Q),
        out_shape=jax.ShapeDtypeStruct((b, s, A_Q), BF16),
        scratch_shapes=[pltpu.VMEM((n_c, tq, tq), jnp.int32), pltpu.VMEM((n_c, tq, tq), F32),
                        pltpu.VMEM((A_HEADS, tq, LANES), BF16), pltpu.VMEM((A_HEADS, n_c, tq, tq), F32),
                        pltpu.VMEM((A_HEADS, SUBLANES, tq), F32), pltpu.VMEM((A_HEADS, SUBLANES, tq), F32),
                        pltpu.VMEM((A_Q, tq), F32)],
        compiler_params=_params("parallel", "arbitrary"),
    )(qb, qib, wt, ki2, kb, vtb)


TPAD = 8
NEW_PAD = LANES
CHUNK_PAGES = 16
PAGE_UNROLL = 8


def _dsa_sample_kernel(t_new, n_pages, kk, pt_ref, qbd_ref, qir_ref, wr_ref, knew_ref, vnew_ref, kinew_ref,
                       ckidx_ref, ck_ref, cv_ref, o_ref,
                       kibuf, kvbuf, logit_ref, key_ref, bias_ref, sem_ki, sem_kv):
    b = pl.program_id(0)
    past = n_pages * PAGE_SIZE
    rows = A_HEADS * TPAD
    n_chunks = n_pages // CHUNK_PAGES

    def ki_copy(p):
        return pltpu.make_async_copy(ckidx_ref.at[pt_ref[b, p]], kibuf.at[p], sem_ki)

    def kv_copy(src_ref, c, i, slot):
        return pltpu.make_async_copy(src_ref.at[pt_ref[b, c * CHUNK_PAGES + i]], kvbuf.at[slot, i], sem_kv.at[slot])

    def start_chunk(src_ref, c, slot):
        lax.fori_loop(0, CHUNK_PAGES, lambda i, _: (kv_copy(src_ref, c, i, slot).start(), 0)[1], 0)

    def wait_chunk(src_ref, c, slot):
        lax.fori_loop(0, CHUNK_PAGES, lambda i, _: (kv_copy(src_ref, c, i, slot).wait(), 0)[1], 0)

    lax.fori_loop(0, n_pages, lambda p, _: (ki_copy(p).start(), 0)[1], 0)
    start_chunk(ck_ref, 0, 0)
    lax.fori_loop(0, n_pages, lambda p, _: (ki_copy(p).wait(), 0)[1], 0)

    qir = qir_ref[...]
    w = wr_ref[...] * (IDX_HEADS ** -0.5)
    tok = lax.broadcasted_iota(jnp.int32, (TPAD, 1), 0)
    tok_ok = tok < t_new

    def keys_of(s, ok):
        t = jnp.maximum(s, 0.0) * w
        return _order_key(jnp.where(ok, jnp.sum(t.reshape(IDX_HEADS, TPAD, LANES), axis=0), NEG_INF))

    def index_page(p, _):
        key_ref[p] = keys_of(_dot(qir, kibuf[p].astype(BF16)), tok_ok)
        return 0

    lax.fori_loop(0, n_pages, index_page, 0, unroll=4)
    new_idx = lax.broadcasted_iota(jnp.int32, (1, NEW_PAD), 1)
    new_ok = jnp.logical_and(jnp.logical_and(new_idx <= tok, new_idx < t_new), tok_ok)
    key_ref[n_pages] = keys_of(_dot_nt(qir, kinew_ref[...]), new_ok)
    bias_ref[0:n_pages] = jnp.broadcast_to(jnp.where(tok_ok, 0.0, NEG_INF)[None], (n_pages, TPAD, LANES))
    bias_ref[n_pages] = jnp.where(new_ok, 0.0, NEG_INF)

    def set_bias(c, val):
        bias_ref[c] = val

    use_topk = jnp.logical_and(past + tok + 1 > kk, tok_ok)
    _select_bias(n_pages + 1, LANES, lambda c: key_ref[c], lambda c: bias_ref[c], set_bias, kk, use_topk)

    qbd = qbd_ref[...]
    for c in range(n_chunks):
        slot = c % 2
        wait_chunk(ck_ref, c, slot)
        if c + 1 < n_chunks:
            start_chunk(ck_ref, c + 1, 1 - slot)
        else:
            start_chunk(cv_ref, 0, 1 - slot)

        def logits_page(i, _, c=c, slot=slot):
            logit_ref[c * CHUNK_PAGES + i] = _dot(qbd, kvbuf[slot, i].astype(BF16))
            return 0

        lax.fori_loop(0, CHUNK_PAGES, logits_page, 0, unroll=PAGE_UNROLL)
    logit_ref[n_pages] = _dot_nt(qbd, knew_ref[...])
    s = logit_ref[...].reshape(n_pages + 1, A_HEADS, TPAD, LANES) + bias_ref[...][:, None]
    s = jnp.where(tok_ok[None, None], s, 0.0).reshape(n_pages + 1, rows, LANES)
    m = jnp.max(jnp.max(s, axis=0), axis=1, keepdims=True)
    e = jnp.exp(s - m[None])
    l = jnp.sum(jnp.sum(e, axis=0), axis=1, keepdims=True)
    logit_ref[...] = e

    acc = _dot(logit_ref[n_pages].astype(BF16), vnew_ref[...])
    for c in range(n_chunks):
        slot = (n_chunks + c) % 2
        wait_chunk(cv_ref, c, slot)
        if c + 1 < n_chunks:
            start_chunk(cv_ref, c + 1, 1 - slot)

        def value_page(i, a, c=c, slot=slot):
            return a + _dot_nt(logit_ref[c * CHUNK_PAGES + i].astype(BF16), kvbuf[slot, i].astype(BF16))

        acc = lax.fori_loop(0, CHUNK_PAGES, value_page, acc, unroll=PAGE_UNROLL)
    acc = (acc / l).reshape(A_HEADS, TPAD, A_Q)
    head_of_lane = lax.broadcasted_iota(jnp.int32, (TPAD, A_Q), 1) // A_HEAD_DIM
    out = jnp.zeros((TPAD, A_Q), F32)
    for hh in range(A_HEADS):
        out = out + jnp.where(head_of_lane == hh, acc[hh], 0.0)
    o_ref[...] = out.astype(o_ref.dtype)


def _dsa_sample(qb, qib, kiw, ki2, kb, vb, cache_kt, cache_vt, cache_kit, page_table, t_new):
    bd, n_pages = page_table.shape
    assert n_pages % CHUNK_PAGES == 0 and t_new <= TPAD
    past = n_pages * PAGE_SIZE
    kk = min(TOPK_MAX, (past + t_new) // 4)
    rows = A_HEADS * TPAD

    def pad_tokens(a, to):
        a = a.reshape(bd, t_new, a.shape[-1])
        return jnp.pad(a, ((0, 0), (0, to - t_new), (0, 0)))

    q8 = pad_tokens(qb, TPAD)
    head_of_lane = jnp.arange(A_Q) // A_HEAD_DIM
    qbd = jnp.where(head_of_lane[None, None, None, :] == jnp.arange(A_HEADS)[None, :, None, None],
                    q8[:, None], jnp.zeros((), BF16)).reshape(bd, rows, A_Q)
    qir = pad_tokens(qib, TPAD).reshape(bd, TPAD, IDX_HEADS, IDX_DIM).transpose(0, 2, 1, 3).reshape(bd, rows, IDX_DIM)
    wr = pad_tokens(kiw[:, IDX_DIM:IDX_DIM + IDX_HEADS], TPAD).transpose(0, 2, 1).reshape(bd, rows, 1)
    knew = pad_tokens(kb, NEW_PAD)
    vnew = pad_tokens(vb, NEW_PAD)
    kinew = pad_tokens(ki2[:, :IDX_DIM], NEW_PAD)

    per_seq = lambda r, c: pl.BlockSpec((None, r, c), lambda b, pt: (b, 0, 0))
    any_spec = pl.BlockSpec(memory_space=pl.ANY)
    grid_spec = pltpu.PrefetchScalarGridSpec(
        num_scalar_prefetch=1,
        grid=(bd,),
        in_specs=[per_seq(rows, A_Q), per_seq(rows, IDX_DIM), per_seq(rows, 1), per_seq(NEW_PAD, A_Q),
                  per_seq(NEW_PAD, A_Q), per_seq(NEW_PAD, IDX_DIM), any_spec, any_spec, any_spec],
        out_specs=per_seq(TPAD, A_Q),
        scratch_shapes=[
            pltpu.VMEM((n_pages, IDX_DIM, PAGE_SIZE), F32),
            pltpu.VMEM((2, CHUNK_PAGES, A_Q, PAGE_SIZE), F32),
            pltpu.VMEM((n_pages + 1, rows, LANES), F32),
            pltpu.VMEM((n_pages + 1, TPAD, LANES), jnp.int32),
            pltpu.VMEM((n_pages + 1, TPAD, LANES), F32),
            pltpu.SemaphoreType.DMA(()),
            pltpu.SemaphoreType.DMA((2,)),
        ],
    )
    out = pl.pallas_call(
        functools.partial(_dsa_sample_kernel, t_new, n_pages, kk),
        grid_spec=grid_spec,
        out_shape=jax.ShapeDtypeStruct((bd, TPAD, A_Q), BF16),
        compiler_params=_params("arbitrary"),
    )(page_table, qbd, qir, wr, knew, vnew, kinew, cache_kit, cache_kt, cache_vt)
    return out[:, :t_new].reshape(bd * t_new, A_Q)


def _glr_kernel(t_valid, layer, bg_ref, lbl_ref, gw_ref, s0_ref, o_ref, s_ref, st_ref):
    tt = bg_ref.shape[0]
    nb = tt // GLR_BLOCK
    it = pl.program_id(1)

    @pl.when(it == 0)
    def _():
        for hh in range(B_HEADS):
            st_ref[hh] = s0_ref[hh].astype(F32).T

    logits = lbl_ref[...]
    ex = jnp.exp(logits - jnp.max(logits, axis=0, keepdims=True))
    lb_all = jnp.sum(ex[:layer + 1], axis=0, keepdims=True) / jnp.sum(ex, axis=0, keepdims=True)

    row = lax.broadcasted_iota(jnp.int32, (tt, 1), 0)
    valid = (it * tt + row) < t_valid
    rloc = row % GLR_BLOCK
    t_idx = lax.broadcasted_iota(jnp.int32, (1, GLR_BLOCK, 1), 1)
    blk_of_col = lax.broadcasted_iota(jnp.int32, (nb, 1, tt), 2) // GLR_BLOCK
    blk_id = lax.broadcasted_iota(jnp.int32, (nb, 1, tt), 0)
    in_block = blk_of_col == blk_id
    gw = gw_ref[...]

    for hh in range(B_HEADS):
        lb = lb_all[:, B_KEY_DIM * hh:B_KEY_DIM * (hh + 1)]
        bq = bg_ref[:, B_KEY_DIM * hh:B_KEY_DIM * (hh + 1)]
        bf = bg_ref[:, B_K + B_KEY_DIM * hh:B_K + B_KEY_DIM * (hh + 1)]
        bi = bg_ref[:, 2 * B_K + B_VAL_DIM * hh:2 * B_K + B_VAL_DIM * (hh + 1)]
        bo = bg_ref[:, 2 * B_K + B_V + B_VAL_DIM * hh:2 * B_K + B_V + B_VAL_DIM * (hh + 1)]
        f = lb + (1.0 - lb) * jax.nn.sigmoid(bf)
        kx = jnp.where(valid, (1.0 - lb) * jax.nn.sigmoid(-bf), 0.0)
        qx = bq * jax.nn.sigmoid(bq)
        cum = jnp.where(valid, jnp.log(f), 0.0)
        for sh in (1, 2, 4, 8):
            cum = cum + jnp.where(rloc >= sh, pltpu.roll(cum, sh, 0), 0.0)
        b3 = cum.reshape(nb, GLR_BLOCK, B_KEY_DIM)
        q3 = qx.reshape(nb, GLR_BLOCK, B_KEY_DIM)
        k3 = kx.reshape(nb, GLR_BLOCK, B_KEY_DIM)
        v3 = bi.reshape(nb, GLR_BLOCK, B_VAL_DIM)
        b_end = b3[:, GLR_BLOCK - 1:GLR_BLOCK, :]
        q_dec = (q3 * jnp.exp(b3)).astype(BF16)
        k_dec = (k3 * jnp.exp(b_end - b3)).reshape(tt, B_KEY_DIM).astype(BF16)

        o_in = jnp.zeros((nb, GLR_BLOCK, B_VAL_DIM), F32)
        for s in range(GLR_BLOCK):
            dec = jnp.exp(jnp.where(t_idx >= s, b3 - b3[:, s:s + 1, :], NEG_INF))
            a = jnp.sum(q3 * k3[:, s:s + 1, :] * dec, axis=-1, keepdims=True)
            o_in = o_in + a * v3[:, s:s + 1, :]

        v_t = bi.T
        lhs = jnp.where(in_block, v_t[None], 0.0).astype(BF16).reshape(nb * B_VAL_DIM, tt)
        u_t = _dot(lhs, k_dec).reshape(nb, B_VAL_DIM, B_KEY_DIM)
        decay_end = jnp.exp(b_end)
        st = st_ref[hh]
        outs = []
        for jb in range(nb):
            outs.append(_dot_nt(q_dec[jb], st.astype(BF16)))
            st = st * decay_end[jb] + u_t[jb]
        st_ref[hh] = st
        o = jnp.concatenate(outs, axis=0) + o_in.reshape(tt, B_VAL_DIM)
        o = _rms(o, gw) * (bo * jax.nn.sigmoid(bo))
        o_ref[:, B_VAL_DIM * hh:B_VAL_DIM * (hh + 1)] = o.astype(o_ref.dtype)

    @pl.when(it == pl.num_programs(1) - 1)
    def _():
        for hh in range(B_HEADS):
            s_ref[hh] = st_ref[hh].T.astype(s_ref.dtype)


def _glr(bg4, lb_logits, glr_norm_w, s0, t_valid, tt, layer):
    b, t, _ = bg4.shape
    state = pl.BlockSpec((None, B_HEADS, B_KEY_DIM, B_VAL_DIM), lambda bi, i: (bi, 0, 0, 0))
    return pl.pallas_call(
        functools.partial(_glr_kernel, t_valid, layer),
        grid=(b, t // tt),
        in_specs=[pl.BlockSpec((None, tt, IN_GROUP_B), lambda bi, i: (bi, i, 0)),
                  pl.BlockSpec(lb_logits.shape, lambda bi, i: (0, 0)),
                  pl.BlockSpec((1, B_VAL_DIM), lambda bi, i: (0, 0)),
                  state],
        out_specs=(pl.BlockSpec((None, tt, B_V), lambda bi, i: (bi, i, 0)), state),
        out_shape=(jax.ShapeDtypeStruct((b, t, B_V), BF16), jax.ShapeDtypeStruct(s0.shape, s0.dtype)),
        scratch_shapes=[pltpu.VMEM((B_HEADS, B_VAL_DIM, B_KEY_DIM), F32)],
        compiler_params=_params("parallel", "arbitrary"),
    )(bg4, lb_logits, glr_norm_w, s0)


def _pool_kernel(n_hist, x_ref, prev_ref, g_ref, wp_ref, sc_ref, o_ref, hist_ref, carry_ref):
    tt, d = x_ref.shape
    group = d // len(POOL_WINDOWS)
    it = pl.program_id(1)

    @pl.when(it == 0)
    def _():
        carry_ref[...] = prev_ref[...]

    x = x_ref[...]
    h = _rms(x, g_ref[...])
    ext = jnp.concatenate([carry_ref[...], h], axis=0)
    pos = it * tt + lax.broadcasted_iota(jnp.int32, (tt, 1), 0) + n_hist
    run = ext
    width = 1
    for gi, win in enumerate(POOL_WINDOWS):
        while width < win:
            run = run + pltpu.roll(run, width, 0)
            width *= 2
        sl = slice(group * gi, group * (gi + 1))
        cnt = jnp.minimum(pos + 1, win).astype(F32)
        z = run[POOL_HALO:, sl] / cnt - h[:, sl]
        y = _dot(z.astype(BF16), wp_ref[gi]) * sc_ref[:, sl]
        o_ref[:, sl] = x[:, sl] + y
    tail = ext[tt:, :]
    carry_ref[...] = tail
    hist_ref[...] = tail


def _pool(x, prev, g, w_pool, scale, n_hist, tt):
    b, t, d = x.shape
    hist_spec = pl.BlockSpec((None, POOL_HALO, d), lambda bi, i: (bi, 0, 0))
    return pl.pallas_call(
        functools.partial(_pool_kernel, n_hist),
        grid=(b, t // tt),
        in_specs=[pl.BlockSpec((None, tt, d), lambda bi, i: (bi, i, 0)), hist_spec,
                  pl.BlockSpec((1, d), lambda bi, i: (0, 0)),
                  pl.BlockSpec(w_pool.shape, lambda bi, i: (0, 0, 0)),
                  pl.BlockSpec((1, d), lambda bi, i: (0, 0))],
        out_specs=(pl.BlockSpec((None, tt, d), lambda bi, i: (bi, i, 0)), hist_spec),
        out_shape=(jax.ShapeDtypeStruct((b, t, d), F32), jax.ShapeDtypeStruct((b, POOL_HALO, d), F32)),
        scratch_shapes=[pltpu.VMEM((POOL_HALO, d), F32)],
        compiler_params=_params("parallel", "arbitrary"),
    )(x, prev, g, w_pool, scale)


def _ffn_ple_kernel(n_chunks, has_mix, final, *refs):
    if has_mix:
        att_ref, glr_ref, wa_ref, wb_ref = refs[:4]
        refs = refs[4:]
    x_ref, p_ref, gf_ref, wg_ref, wu_ref, wd_ref, gp_ref, wpg_ref, wpe_ref, gfin_ref, o_ref = refs
    x = x_ref[...]
    if has_mix:
        x = x + _dot(att_ref[...], wa_ref[...]) + _dot(glr_ref[...], wb_ref[...])
    h = _rms(x, gf_ref[...]).astype(BF16)
    d_ff = wg_ref.shape[1]
    cw = d_ff // n_chunks
    acc = None
    for c in range(n_chunks):
        sl = slice(cw * c, cw * (c + 1))
        gate = _dot(h, wg_ref[:, sl])
        up = _dot(h, wu_ref[:, sl])
        part = _dot((gate * jax.nn.sigmoid(gate) * up).astype(BF16), wd_ref[sl, :])
        acc = part if acc is None else acc + part
    x = x + acc
    gate = jax.nn.sigmoid(_dot(_rms(x, gp_ref[...]).astype(BF16), wpg_ref[...]))
    x = x + _dot(p_ref[...].astype(BF16), wpe_ref[...]) * gate
    if final:
        x = _rms(x, gfin_ref[...])
    o_ref[...] = x


def _ffn_ple(x, mix, p, g_ffn, w_gate, w_up, w_down, g_ple, w_ple_gate, w_ple, g_final, final, tm):
    n, d = x.shape
    d_ff = w_gate.shape[1]
    n_chunks = 2 if d_ff % (2 * LANES) == 0 else 1
    row = lambda width: pl.BlockSpec((tm, width), lambda i: (i, 0))
    args = [x, p, g_ffn, w_gate, w_up, w_down, g_ple, w_ple_gate, w_ple, g_final]
    specs = [row(d), row(p.shape[1])] + [_const_spec(a.shape) for a in args[2:]]
    if mix is not None:
        args = list(mix) + args
        specs = [row(mix[0].shape[1]), row(mix[1].shape[1]), _const_spec(mix[2].shape), _const_spec(mix[3].shape)] + specs
    return pl.pallas_call(
        functools.partial(_ffn_ple_kernel, n_chunks, mix is not None, final),
        grid=(n // tm,),
        in_specs=specs,
        out_specs=row(d),
        out_shape=jax.ShapeDtypeStruct((n, d), F32),
        compiler_params=_params("parallel"),
    )(*args)


def _rope_tables(pos):
    half = A_HEAD_DIM // 2
    inv = ROPE_THETA ** (-jnp.arange(half, dtype=F32) / half)
    ang = pos.astype(F32)[:, None] * inv[None, :]
    c, s = jnp.cos(ang), jnp.sin(ang)
    reps = LANES // A_HEAD_DIM
    return jnp.tile(jnp.concatenate([c, c], axis=1), (1, reps)), jnp.tile(jnp.concatenate([-s, s], axis=1), (1, reps))


def _pack_w_in(w):
    d = w.shape[0]
    n_a = 3 * A_Q + IDX_Q
    n_i = IDX_DIM + IDX_HEADS
    return jnp.concatenate([w[:, :n_a], w[:, n_a:n_a + n_i], jnp.zeros((d, IN_GROUP_I - n_i), w.dtype),
                            w[:, n_a + n_i:]], axis=1).astype(BF16)


def _row_tile(n, want):
    t = min(n, want)
    assert n % t == 0
    return t


def _trunk(x, p, pos, attend, glr_s0, glr_tile, pool_prev, pool_hist, W):
    b, t, d = x.shape
    n = b * t
    tm = _row_tile(n, 256)
    row2 = lambda a: a.reshape(1, -1)
    cos_tab, sin_tab = _rope_tables(pos)
    depth = W["mix_norm"].shape[0]
    xf = x.reshape(n, d)
    outs = {}
    for i in range(depth):
        jj = i // 2
        if i % 2 == 0:
            bs, ts = (b, t) if t % tm == 0 else (1, n)
            res = _in_proj(xf.reshape(bs, ts, d), row2(W["mix_norm"][i]), _pack_w_in(W["w_in"][jj]),
                           cos_tab, sin_tab, tm)
            flat = lambda a: a.reshape(n, a.shape[2])
            qb, kt, kb, vt, vb, vtb, qib, kiw, kit, wt, ki2, bg4 = res
            qb, kb, vb, qib, kiw, ki2, bg4 = [flat(a) for a in (qb, kb, vb, qib, kiw, ki2, bg4)]
            att = attend(jj, qb, qib, kiw, ki2, kb, vb, wt, vtb)
            t_pad = -(-t // glr_tile) * glr_tile
            bg3 = bg4.reshape(b, t, IN_GROUP_B)
            if t_pad != t:
                bg3 = jnp.pad(bg3, ((0, 0), (0, t_pad - t), (0, 0)))
            glr, s_new = _glr(bg3, W["lb_logits"], row2(W["glr_norm"][jj]), glr_s0[jj], t, glr_tile, jj)
            glr = glr[:, :t].reshape(n, B_V)
            w_out = W["w_out"][jj].astype(BF16)
            mix = (att, glr, w_out[:A_Q], w_out[A_Q:])
            heads = lambda a: a.reshape(bs, A_HEADS, A_HEAD_DIM, ts).transpose(0, 3, 1, 2).reshape(
                b, t, A_HEADS, A_HEAD_DIM)
            outs.setdefault("k", []).append(heads(kt))
            outs.setdefault("v", []).append(heads(vt))
            outs.setdefault("ki", []).append(kit.transpose(0, 2, 1).reshape(b, t, IDX_DIM))
            outs.setdefault("s", []).append(s_new)
        else:
            tt = _row_tile(t, 256)
            t_pad = -(-t // 8) * 8
            x3 = xf.reshape(b, t, d)
            if t_pad != t:
                x3 = jnp.pad(x3, ((0, 0), (0, t_pad - t), (0, 0)))
                tt = t_pad
            y3, hist = _pool(x3, pool_prev[jj], row2(W["mix_norm"][i]), W["pool_w"][jj].astype(BF16),
                             row2(W["pool_scale"][jj]), pool_hist, tt)
            xf = y3[:, :t].reshape(n, d)
            mix = None
            outs.setdefault("hist", []).append((hist, t_pad))
        d_ff = W["w_down"].shape[1]
        w_up = W["w_up"][i].astype(BF16)
        xf = _ffn_ple(xf, mix, p[i].reshape(n, -1), row2(W["ffn_norm"][i]), w_up[:, :d_ff], w_up[:, d_ff:],
                      W["w_down"][i].astype(BF16), row2(W["ple_norm"][i]), W["w_ple_gate"][i].astype(BF16),
                      W["w_ple"][i].astype(BF16), row2(W["final_norm"]), i == depth - 1, tm)
    return xf.reshape(b, t, d), outs


def kernel(x_prompt, x_sample, cache_k, cache_v, cache_kidx, state_glr, state_pool, page_table, p_prompt, p_sample,
           mix_norm, w_in, w_out, lb_logits, glr_norm, pool_w, pool_scale, ffn_norm, w_up, w_down, ple_norm,
           w_ple_gate, w_ple, final_norm):
    W = dict(mix_norm=mix_norm, w_in=w_in, w_out=w_out, lb_logits=lb_logits, glr_norm=glr_norm, pool_w=pool_w,
             pool_scale=pool_scale, ffn_norm=ffn_norm, w_up=w_up, w_down=w_down, ple_norm=ple_norm,
             w_ple_gate=w_ple_gate, w_ple=w_ple, final_norm=final_norm)
    n_ab = w_in.shape[0]
    n_c = pool_w.shape[0]
    d = x_prompt.shape[-1]
    n_buf = max(POOL_WINDOWS) - 1

    bp, sp = x_prompt.shape[:2]

    def attend_prompt(jj, qb, qib, kiw, ki2, kb, vb, wt, vtb):
        r3 = lambda a: a.reshape(bp, sp, a.shape[-1])
        return _dsa_prompt(r3(qb), r3(qib), wt, r3(ki2), r3(kb), vtb, vtb.shape[-1]).reshape(bp * sp, A_Q)

    glr0_p = [jnp.zeros((bp, B_HEADS, B_KEY_DIM, B_VAL_DIM), state_glr.dtype) for _ in range(n_ab)]
    pool0_p = [jnp.zeros((bp, POOL_HALO, d), F32) for _ in range(n_c)]
    yp, op = _trunk(x_prompt, p_prompt, jnp.arange(sp, dtype=jnp.int32), attend_prompt, glr0_p,
                    _row_tile(sp, 256), pool0_p, 0, W)

    bd, ts = x_sample.shape[:2]
    past = page_table.shape[1] * PAGE_SIZE

    n_pool = cache_k.shape[1]
    cache_kt = jnp.transpose(cache_k, (0, 1, 3, 4, 2)).reshape(n_ab, n_pool, A_Q, PAGE_SIZE)
    cache_vt = jnp.transpose(cache_v, (0, 1, 3, 4, 2)).reshape(n_ab, n_pool, A_Q, PAGE_SIZE)
    cache_kit = jnp.transpose(cache_kidx, (0, 1, 3, 2))

    def attend_sample(jj, qb, qib, kiw, ki2, kb, vb, wt, vtb):
        return _dsa_sample(qb, qib, kiw, ki2, kb, vb, cache_kt[jj], cache_vt[jj], cache_kit[jj], page_table, ts)

    pos_s = jnp.tile(past + jnp.arange(ts, dtype=jnp.int32), bd)
    pool0_s = [jnp.pad(state_pool[j].astype(F32), ((0, 0), (POOL_HALO - n_buf, 0), (0, 0))) for j in range(n_c)]
    ys, os_ = _trunk(x_sample, p_sample, pos_s, attend_sample, [state_glr[j] for j in range(n_ab)],
                     LANES, pool0_s, n_buf, W)

    def pool_rows(o, prev, t):
        res = []
        for j, (hist, t_pad) in enumerate(o["hist"]):
            if t >= n_buf:
                res.append(hist[:, POOL_HALO - (t_pad - t) - n_buf:POOL_HALO - (t_pad - t)])
            else:
                new = hist[:, POOL_HALO - t_pad:POOL_HALO - t_pad + t]
                res.append(jnp.concatenate([prev[j][:, t:].astype(new.dtype), new], axis=1))
        return jnp.stack(res)

    return (yp, ys, jnp.stack(op["k"]), jnp.stack(op["v"]), jnp.stack(op["ki"]), jnp.stack(op["s"]),
            pool_rows(op, None, sp).astype(x_prompt.dtype),
            jnp.stack(os_["k"]), jnp.stack(os_["v"]), jnp.stack(os_["ki"]), jnp.stack(os_["s"]),
            pool_rows(os_, state_pool, ts).astype(x_sample.dtype))
```

```python
import functools

import jax
import jax.numpy as jnp
from jax import lax
from jax.experimental import pallas as pl
from jax.experimental.pallas import tpu as pltpu

F32 = jnp.float32
BF16 = jnp.bfloat16

A_HEADS = 8
A_HEAD_DIM = 64
IDX_HEADS = 8
IDX_DIM = 64
TOPK_MAX = 256
ROPE_THETA = 10000.0
B_HEADS = 4
B_KEY_DIM = 128
B_VAL_DIM = 128
GLR_BLOCK = 16
POOL_WINDOWS = (2, 4, 8, 16)
POOL_HALO = 16
PAGE_SIZE = 128
RMS_EPS = 1e-6

A_Q = A_HEADS * A_HEAD_DIM
IDX_Q = IDX_HEADS * IDX_DIM
B_K = B_HEADS * B_KEY_DIM
B_V = B_HEADS * B_VAL_DIM

LANES = 128
VMEM_LIMIT = 56 * 1024 * 1024
INT_MIN = -(2 ** 31)
NEG_INF = float("-inf")


def _params(*sem):
    return pltpu.CompilerParams(dimension_semantics=sem, vmem_limit_bytes=VMEM_LIMIT)


def _rms(x, g):
    ms = jnp.mean(x * x, axis=-1, keepdims=True)
    return x * lax.rsqrt(ms + RMS_EPS) * g


def _dot(a, b):
    return jnp.dot(a, b, preferred_element_type=F32)


def _dot_nt(a, b):
    return lax.dot_general(a, b, (((1,), (1,)), ((), ())), preferred_element_type=F32)


def _const_spec(shape):
    nd = len(shape)
    return pl.BlockSpec(shape, lambda *_: (0,) * nd, pipeline_mode=pl.Buffered(1))


IN_GROUP_A = 4 * A_Q
IN_GROUP_I = LANES
IN_GROUP_B = 2 * B_K + 2 * B_V
IN_PACKED = IN_GROUP_A + IN_GROUP_I + IN_GROUP_B


def _in_proj_kernel(x_ref, g_ref, w_ref, cos_ref, sin_ref,
                    qb_ref, kt_ref, kb_ref, vt_ref, vb_ref, vtb_ref, qib_ref, kiw_ref, kit_ref, wt_ref, ki2_ref,
                    bg_ref):
    tm = x_ref.shape[0]
    h = _rms(x_ref[...], g_ref[...]).astype(BF16)
    cos = cos_ref[...]
    sin = sin_ref[...]
    lane = lax.broadcasted_iota(jnp.int32, (tm, LANES), 1)
    first = (lane % A_HEAD_DIM) < (A_HEAD_DIM // 2)
    slab = lambda i: slice(LANES * i, LANES * (i + 1))

    def rope(z, c, s):
        rot = jnp.where(first, pltpu.roll(z, LANES - A_HEAD_DIM // 2, 1), pltpu.roll(z, A_HEAD_DIM // 2, 1))
        return z * c + rot * s

    def roped(col0):
        z = _dot(h, w_ref[:, col0:col0 + A_Q])
        return [rope(z[:, slab(i)], cos, sin) for i in range(A_Q // LANES)]

    scale = A_HEAD_DIM ** -0.5
    for i, r in enumerate(roped(0)):
        qb_ref[:, slab(i)] = (r * scale).astype(BF16)
    for i, r in enumerate(roped(A_Q)):
        kt_ref[slab(i), :] = r.T
        kb_ref[:, slab(i)] = r.astype(BF16)
    v = _dot(h, w_ref[:, 2 * A_Q:3 * A_Q])
    for i in range(A_Q // LANES):
        v_t = v[:, slab(i)].T
        vt_ref[slab(i), :] = v_t
        vtb_ref[slab(i), :] = v_t.astype(BF16)
    vb_ref[...] = v.astype(BF16)
    for i, r in enumerate(roped(3 * A_Q)):
        qib_ref[:, slab(i)] = (r * (IDX_DIM ** -0.5)).astype(BF16)
    is_key = lane < IDX_DIM
    z = _dot(h, w_ref[:, IN_GROUP_A:IN_GROUP_A + IN_GROUP_I])
    kiw = rope(z, jnp.where(is_key, cos, 1.0), jnp.where(is_key, sin, 0.0))
    kiw_ref[...] = kiw
    kiw_t = kiw.T
    kit_ref[...] = kiw_t[:IDX_DIM, :]
    wt_ref[...] = kiw_t[IDX_DIM:IDX_DIM + IDX_HEADS, :]
    ki2_ref[...] = jnp.where(is_key, kiw, pltpu.roll(kiw, IDX_DIM, 1)).astype(BF16)
    col0 = IN_GROUP_A + IN_GROUP_I
    for i in range(IN_GROUP_B // A_Q):
        bg_ref[:, A_Q * i:A_Q * (i + 1)] = _dot(h, w_ref[:, col0 + A_Q * i:col0 + A_Q * (i + 1)])


def _in_proj(x, g, w_packed, cos_tab, sin_tab, tm):
    b, t, d = x.shape
    row = lambda width: pl.BlockSpec((None, tm, width), lambda bi, i: (bi, i, 0))
    col = lambda height: pl.BlockSpec((None, height, tm), lambda bi, i: (bi, 0, i))
    tab = pl.BlockSpec((tm, LANES), lambda bi, i: (i, 0))
    rows = lambda width, dt: (jax.ShapeDtypeStruct((b, t, width), dt), row(width))
    cols = lambda height: (jax.ShapeDtypeStruct((b, height, t), F32), col(height))
    outs = (
        rows(A_Q, BF16),
        cols(A_Q),
        rows(A_Q, BF16),
        cols(A_Q),
        rows(A_Q, BF16),
        (jax.ShapeDtypeStruct((b, t // tm, A_Q, tm), BF16),
         pl.BlockSpec((None, None, A_Q, tm), lambda bi, i: (bi, i, 0, 0))),
        rows(IDX_Q, BF16),
        rows(LANES, F32),
        cols(IDX_DIM),
        cols(IDX_HEADS),
        rows(LANES, BF16),
        rows(IN_GROUP_B, F32),
    )
    return pl.pallas_call(
        _in_proj_kernel,
        grid=(b, t // tm),
        in_specs=[row(d), _const_spec((1, d)), _const_spec((d, IN_PACKED)), tab, tab],
        out_specs=tuple(o[1] for o in outs),
        out_shape=tuple(o[0] for o in outs),
        compiler_params=_params("parallel", "parallel"),
    )(x, g, w_packed, cos_tab, sin_tab)


def _order_key(x):
    bits = pltpu.bitcast(x + 0.0, jnp.int32)
    return bits ^ ((bits >> 31) & jnp.int32(0x7FFFFFFF))


def _unrolled(n, body, init):
    for c in range(n):
        init = body(c, init)
    return init


def _fold_lanes(t):
    acc = t[:, :LANES]
    for i in range(1, t.shape[1] // LANES):
        acc = acc + t[:, LANES * i:LANES * (i + 1)]
    return acc


def _select_bias(n_c, width, get_key, get_bias, set_bias, kk, use_topk):
    rows = use_topk.shape[0]
    kf = float(kk)
    zeros = jnp.zeros((rows, LANES), F32)

    def count(cmp, level):
        level_b = jnp.broadcast_to(level, (rows, LANES))
        parts = [zeros] * 4
        for c in range(n_c):
            k = get_key(c)
            for i in range(width // LANES):
                hit = jnp.where(cmp(k[:, LANES * i:LANES * (i + 1)], level_b), 1.0, 0.0)
                parts[(c + i) % 4] = parts[(c + i) % 4] + hit
        return jnp.sum((parts[0] + parts[1]) + (parts[2] + parts[3]), axis=1, keepdims=True)

    def bisect(i, t):
        cand = t | jnp.left_shift(jnp.int32(1), 31 - i)
        return jnp.where(count(jnp.greater_equal, cand ^ jnp.int32(INT_MIN)) >= kf, cand, t)

    thr = lax.fori_loop(0, 32, bisect, jnp.zeros((rows, 1), jnp.int32)) ^ jnp.int32(INT_MIN)

    def apply(c, acc):
        ge = get_key(c) >= thr
        set_bias(c, jnp.where(use_topk, jnp.where(ge, 0.0, NEG_INF), get_bias(c)))
        return acc + _fold_lanes(jnp.where(ge, 1.0, 0.0))

    cnt_ge = jnp.sum(_unrolled(n_c, apply, zeros), axis=1, keepdims=True)
    tied = jnp.logical_and(use_topk, cnt_ge != kf)

    @pl.when(jnp.max(jnp.where(tied, 1.0, 0.0)) > 0.0)
    def _():
        room = kf - count(jnp.greater, thr)
        before = (lax.broadcasted_iota(jnp.int32, (width, width), 0)
                  < lax.broadcasted_iota(jnp.int32, (width, width), 1))
        tri = jnp.where(before, 1.0, 0.0).astype(BF16)

        def fix(c, seen):
            kc = get_key(c)
            eq = jnp.where(kc == thr, 1.0, 0.0)
            rank = _dot(eq.astype(BF16), tri) + seen
            keep = jnp.logical_or(kc > thr, jnp.logical_and(kc == thr, rank < room))
            set_bias(c, jnp.where(use_topk, jnp.where(keep, 0.0, NEG_INF), get_bias(c)))
            return seen + jnp.sum(eq, axis=1, keepdims=True)

        _unrolled(n_c, fix, jnp.zeros((rows, 1), F32))


def _pair_masks(rows):
    lane = lax.broadcasted_iota(jnp.int32, (rows, LANES), 1)
    lo = lane < A_HEAD_DIM
    return lo, jnp.logical_not(lo)


M_INIT = -1e30


SUBLANES = 8
LOG2_E = 1.4426950408889634


def _fold_rows(a, op):
    return op(a.reshape(a.shape[0] // SUBLANES, SUBLANES, a.shape[1]), axis=0)


def _select_bias_t(n_c, key_ref, bias_ref, kk, use_topk):
    kc, rows = key_ref.shape[1:]
    kf = float(kk)
    zeros = jnp.zeros((SUBLANES, rows), F32)

    def count(cmp, level):
        level8 = jnp.broadcast_to(level, (SUBLANES, rows))[None, None]

        def body(c, a):
            k4 = key_ref[c].reshape(4, kc // (4 * SUBLANES), SUBLANES, rows)
            part = jnp.sum(jnp.where(cmp(k4, level8), 1.0, 0.0), axis=1)
            return a + ((part[0] + part[1]) + (part[2] + part[3]))

        return jnp.sum(lax.fori_loop(0, n_c, body, zeros), axis=0, keepdims=True)

    def bisect(i, t):
        cand = t | jnp.left_shift(jnp.int32(1), 31 - i)
        return jnp.where(count(jnp.greater_equal, cand ^ jnp.int32(INT_MIN)) >= kf, cand, t)

    thr = lax.fori_loop(0, 32, bisect, jnp.zeros((1, rows), jnp.int32)) ^ jnp.int32(INT_MIN)

    def apply(c, acc):
        ge = key_ref[c] >= thr
        bias_ref[c] = jnp.where(use_topk, jnp.where(ge, 0.0, NEG_INF), bias_ref[c])
        return acc + _fold_rows(jnp.where(ge, 1.0, 0.0), jnp.sum)

    cnt_ge = jnp.sum(lax.fori_loop(0, n_c, apply, zeros), axis=0, keepdims=True)
    tied = jnp.logical_and(use_topk, cnt_ge != kf)

    @pl.when(jnp.max(jnp.where(tied, 1.0, 0.0)) > 0.0)
    def _():
        room = kf - count(jnp.greater, thr)
        earlier = (lax.broadcasted_iota(jnp.int32, (kc, kc), 1) < lax.broadcasted_iota(jnp.int32, (kc, kc), 0))
        tri = jnp.where(earlier, 1.0, 0.0).astype(BF16)

        def fix(c, seen):
            k = key_ref[c]
            eq = jnp.where(k == thr, 1.0, 0.0)
            rank = _dot(tri, eq.astype(BF16)) + seen
            keep = jnp.logical_or(k > thr, jnp.logical_and(k == thr, rank < room))
            bias_ref[c] = jnp.where(use_topk, jnp.where(keep, 0.0, NEG_INF), bias_ref[c])
            return seen + jnp.sum(eq, axis=0, keepdims=True)

        lax.fori_loop(0, n_c, fix, jnp.zeros((1, rows), F32))


def _dsa_prompt_kernel(kk, q_ref, qi_ref, wt_ref, ki2_ref, k_ref, vt_ref, o_ref,
                       key_ref, bias_ref, qh_ref, s_ref, max_ref, sum_ref, acc_ref):
    tq = q_ref.shape[0]
    kc = key_ref.shape[1]
    j = pl.program_id(1)
    n_c = j + 1
    qpos = j * tq + lax.broadcasted_iota(jnp.int32, (1, tq), 1)
    krow = lax.broadcasted_iota(jnp.int32, (kc, 1), 0)
    masks = _pair_masks(tq)
    slab = lambda p: slice(LANES * p, LANES * (p + 1))
    head_rows = lambda hh: slice(A_HEAD_DIM * hh, A_HEAD_DIM * (hh + 1))
    keys_of = lambda c: pl.ds(pl.multiple_of(c * kc, kc), kc)
    one_head = lambda ref, hh: jnp.where(masks[hh % 2], ref[:, slab(hh // 2)], jnp.zeros((), ref.dtype))
    causal = lambda c: c * kc + krow <= qpos

    def init_bias(c, _):
        bias_ref[c] = jnp.where(causal(c), 0.0, NEG_INF)
        return 0

    lax.fori_loop(0, n_c, init_bias, 0)

    @pl.when(n_c * kc > kk)
    def _():
        w = wt_ref[...] * (IDX_HEADS ** -0.5)
        for hh in range(IDX_HEADS):
            qh_ref[hh] = one_head(qi_ref, hh)

        def score_chunk(c, _):
            ki2 = ki2_ref[keys_of(c), :]
            acc = None
            for hh in range(IDX_HEADS):
                term = jnp.maximum(_dot_nt(ki2, qh_ref[hh]), 0.0) * w[hh:hh + 1, :]
                acc = term if acc is None else acc + term
            key_ref[c] = _order_key(jnp.where(causal(c), acc, NEG_INF))
            return 0

        lax.fori_loop(0, n_c, score_chunk, 0)
        _select_bias_t(n_c, key_ref, bias_ref, kk, qpos >= kk)

    for hh in range(A_HEADS):
        qh_ref[hh] = one_head(q_ref, hh)
        max_ref[hh] = jnp.full((SUBLANES, tq), M_INIT, F32)
        sum_ref[hh] = jnp.zeros((SUBLANES, tq), F32)
    acc_ref[...] = jnp.zeros(acc_ref.shape, F32)

    def max_chunk(c, _):
        bias = bias_ref[c]
        for hh in range(A_HEADS):
            s = _dot_nt(k_ref[keys_of(c), slab(hh // 2)], qh_ref[hh]) * LOG2_E + bias
            s_ref[hh, c] = s
            max_ref[hh] = jnp.maximum(max_ref[hh], _fold_rows(s, jnp.max))
        return 0

    lax.fori_loop(0, n_c, max_chunk, 0)
    for hh in range(A_HEADS):
        max_ref[hh] = jnp.broadcast_to(jnp.max(max_ref[hh], axis=0, keepdims=True), (SUBLANES, tq))

    def sum_chunk(c, _):
        for hh in range(A_HEADS):
            e = jnp.exp2(s_ref[hh, c].reshape(kc // SUBLANES, SUBLANES, tq) - max_ref[hh][None])
            sum_ref[hh] = sum_ref[hh] + jnp.sum(e, axis=0)
            pv = _dot(vt_ref[c, head_rows(hh), :], e.reshape(kc, tq).astype(BF16))
            acc_ref[head_rows(hh), :] = acc_ref[head_rows(hh), :] + pv
        return 0

    lax.fori_loop(0, n_c, sum_chunk, 0)
    for p in range(A_Q // LANES):
        parts = [acc_ref[head_rows(hh), :] / jnp.sum(sum_ref[hh], axis=0, keepdims=True) for hh in (2 * p, 2 * p + 1)]
        o_ref[:, slab(p)] = jnp.concatenate(parts, axis=0).T.astype(o_ref.dtype)


def _dsa_prompt(qb, qib, wt, ki2, kb, vtb, tq):
    b, s, _ = qb.shape
    n_c = s // tq
    assert vtb.shape == (b, n_c, A_Q, tq)
    kk = min(TOPK_MAX, s // 4)
    tile = lambda width: pl.BlockSpec((None, tq, width), lambda bi, j: (bi, j, 0))
    full = lambda width: pl.BlockSpec((None, s, width), lambda bi, j: (bi, 0, 0))
    return pl.pallas_call(
        functools.partial(_dsa_prompt_kernel, kk),
        grid=(b, n_c),
        in_specs=[tile(A_Q), tile(IDX_Q), pl.BlockSpec((None, IDX_HEADS, tq), lambda bi, j: (bi, 0, j)),
                  full(LANES), full(A_Q), pl.BlockSpec((None, n_c, A_Q, tq), lambda bi, j: (bi, 0, 0, 0))],
        out_specs=tile(A_Q),
        out_shape=jax.ShapeDtypeStruct((b, s, A_Q), BF16),
        scratch_shapes=[pltpu.VMEM((n_c, tq, tq), jnp.int32), pltpu.VMEM((n_c, tq, tq), F32),
                        pltpu.VMEM((A_HEADS, tq, LANES), BF16), pltpu.VMEM((A_HEADS, n_c, tq, tq), F32),
                        pltpu.VMEM((A_HEADS, SUBLANES, tq), F32), pltpu.VMEM((A_HEADS, SUBLANES, tq), F32),
                        pltpu.VMEM((A_Q, tq), F32)],
        compiler_params=_params("parallel", "arbitrary"),
    )(qb, qib, wt, ki2, kb, vtb)


TPAD = 8
NEW_PAD = LANES
CHUNK_PAGES = 16
PAGE_UNROLL = 8
RING_SLOTS = 6


def _dsa_sample_kernel(t_new, n_pages, kk, pt_ref, qbd_ref, qir_ref, wr_ref, knew_ref, vnew_ref, kinew_ref,
                       ckidx_ref, ck_ref, cv_ref, o_ref,
                       kibuf, kvbuf, logit_ref, key_ref, bias_ref, sem_ki, sem_kv):
    b = pl.program_id(0)
    past = n_pages * PAGE_SIZE
    rows = A_HEADS * TPAD
    n_chunks = n_pages // CHUNK_PAGES

    def ki_copy(p):
        return pltpu.make_async_copy(ckidx_ref.at[pt_ref[b, p]], kibuf.at[p], sem_ki)

    def kv_copy(src_ref, c, i, slot):
        return pltpu.make_async_copy(src_ref.at[pt_ref[b, c * CHUNK_PAGES + i]], kvbuf.at[slot, i], sem_kv.at[slot])

    def start_chunk(src_ref, c, slot):
        lax.fori_loop(0, CHUNK_PAGES, lambda i, _: (kv_copy(src_ref, c, i, slot).start(), 0)[1], 0)

    def wait_chunk(src_ref, c, slot):
        lax.fori_loop(0, CHUNK_PAGES, lambda i, _: (kv_copy(src_ref, c, i, slot).wait(), 0)[1], 0)

    n_xfer = 2 * n_chunks

    def xfer(t):
        return (ck_ref, t, t % RING_SLOTS) if t < n_chunks else (cv_ref, t - n_chunks, t % RING_SLOTS)

    lax.fori_loop(0, n_pages, lambda p, _: (ki_copy(p).start(), 0)[1], 0)
    for t in range(min(RING_SLOTS, n_xfer)):
        start_chunk(*xfer(t))
    lax.fori_loop(0, n_pages, lambda p, _: (ki_copy(p).wait(), 0)[1], 0)

    qir = qir_ref[...]
    w = wr_ref[...] * (IDX_HEADS ** -0.5)
    tok = lax.broadcasted_iota(jnp.int32, (TPAD, 1), 0)
    tok_ok = tok < t_new

    def keys_of(s, ok):
        t = jnp.maximum(s, 0.0) * w
        return _order_key(jnp.where(ok, jnp.sum(t.reshape(IDX_HEADS, TPAD, LANES), axis=0), NEG_INF))

    def index_page(p, _):
        key_ref[p] = keys_of(_dot(qir, kibuf[p].astype(BF16)), tok_ok)
        return 0

    lax.fori_loop(0, n_pages, index_page, 0, unroll=PAGE_UNROLL)
    new_idx = lax.broadcasted_iota(jnp.int32, (1, NEW_PAD), 1)
    new_ok = jnp.logical_and(jnp.logical_and(new_idx <= tok, new_idx < t_new), tok_ok)
    key_ref[n_pages] = keys_of(_dot_nt(qir, kinew_ref[...]), new_ok)
    bias_ref[0:n_pages] = jnp.broadcast_to(jnp.where(tok_ok, 0.0, NEG_INF)[None], (n_pages, TPAD, LANES))
    bias_ref[n_pages] = jnp.where(new_ok, 0.0, NEG_INF)

    def set_bias(c, val):
        bias_ref[c] = val

    use_topk = jnp.logical_and(past + tok + 1 > kk, tok_ok)
    _select_bias(n_pages + 1, LANES, lambda c: key_ref[c], lambda c: bias_ref[c], set_bias, kk, use_topk)

    qbd = qbd_ref[...]
    for c in range(n_chunks):
        slot = c % RING_SLOTS
        wait_chunk(*xfer(c))

        def logits_page(i, _, c=c, slot=slot):
            logit_ref[c * CHUNK_PAGES + i] = _dot(qbd, kvbuf[slot, i].astype(BF16))
            return 0

        lax.fori_loop(0, CHUNK_PAGES, logits_page, 0, unroll=PAGE_UNROLL)
        if c + RING_SLOTS < n_xfer:
            start_chunk(*xfer(c + RING_SLOTS))
    logit_ref[n_pages] = _dot_nt(qbd, knew_ref[...])
    s = logit_ref[...].reshape(n_pages + 1, A_HEADS, TPAD, LANES) + bias_ref[...][:, None]
    s = jnp.where(tok_ok[None, None], s, 0.0).reshape(n_pages + 1, rows, LANES)
    m = jnp.max(jnp.max(s, axis=0), axis=1, keepdims=True)
    e = jnp.exp(s - m[None])
    l = jnp.sum(jnp.sum(e, axis=0), axis=1, keepdims=True)
    logit_ref[...] = e

    acc = _dot(logit_ref[n_pages].astype(BF16), vnew_ref[...])
    for c in range(n_chunks):
        slot = (n_chunks + c) % RING_SLOTS
        wait_chunk(*xfer(n_chunks + c))

        def value_page(i, a, c=c, slot=slot):
            return a + _dot_nt(logit_ref[c * CHUNK_PAGES + i].astype(BF16), kvbuf[slot, i].astype(BF16))

        acc = lax.fori_loop(0, CHUNK_PAGES, value_page, acc, unroll=PAGE_UNROLL)
        if n_chunks + c + RING_SLOTS < n_xfer:
            start_chunk(*xfer(n_chunks + c + RING_SLOTS))
    acc = (acc / l).reshape(A_HEADS, TPAD, A_Q)
    head_of_lane = lax.broadcasted_iota(jnp.int32, (TPAD, A_Q), 1) // A_HEAD_DIM
    out = jnp.zeros((TPAD, A_Q), F32)
    for hh in range(A_HEADS):
        out = out + jnp.where(head_of_lane == hh, acc[hh], 0.0)
    o_ref[...] = out.astype(o_ref.dtype)


def _dsa_sample(qb, qib, kiw, ki2, kb, vb, cache_kt, cache_vt, cache_kit, page_table, t_new):
    bd, n_pages = page_table.shape
    assert n_pages % CHUNK_PAGES == 0 and t_new <= TPAD
    past = n_pages * PAGE_SIZE
    kk = min(TOPK_MAX, (past + t_new) // 4)
    rows = A_HEADS * TPAD

    def pad_tokens(a, to):
        a = a.reshape(bd, t_new, a.shape[-1])
        return jnp.pad(a, ((0, 0), (0, to - t_new), (0, 0)))

    q8 = pad_tokens(qb, TPAD)
    head_of_lane = jnp.arange(A_Q) // A_HEAD_DIM
    qbd = jnp.where(head_of_lane[None, None, None, :] == jnp.arange(A_HEADS)[None, :, None, None],
                    q8[:, None], jnp.zeros((), BF16)).reshape(bd, rows, A_Q)
    qir = pad_tokens(qib, TPAD).reshape(bd, TPAD, IDX_HEADS, IDX_DIM).transpose(0, 2, 1, 3).reshape(bd, rows, IDX_DIM)
    wr = pad_tokens(kiw[:, IDX_DIM:IDX_DIM + IDX_HEADS], TPAD).transpose(0, 2, 1).reshape(bd, rows, 1)
    knew = pad_tokens(kb, NEW_PAD)
    vnew = pad_tokens(vb, NEW_PAD)
    kinew = pad_tokens(ki2[:, :IDX_DIM], NEW_PAD)

    per_seq = lambda r, c: pl.BlockSpec((None, r, c), lambda b, pt: (b, 0, 0))
    any_spec = pl.BlockSpec(memory_space=pl.ANY)
    grid_spec = pltpu.PrefetchScalarGridSpec(
        num_scalar_prefetch=1,
        grid=(bd,),
        in_specs=[per_seq(rows, A_Q), per_seq(rows, IDX_DIM), per_seq(rows, 1), per_seq(NEW_PAD, A_Q),
                  per_seq(NEW_PAD, A_Q), per_seq(NEW_PAD, IDX_DIM), any_spec, any_spec, any_spec],
        out_specs=per_seq(TPAD, A_Q),
        scratch_shapes=[
            pltpu.VMEM((n_pages, IDX_DIM, PAGE_SIZE), F32),
            pltpu.VMEM((RING_SLOTS, CHUNK_PAGES, A_Q, PAGE_SIZE), F32),
            pltpu.VMEM((n_pages + 1, rows, LANES), F32),
            pltpu.VMEM((n_pages + 1, TPAD, LANES), jnp.int32),
            pltpu.VMEM((n_pages + 1, TPAD, LANES), F32),
            pltpu.SemaphoreType.DMA(()),
            pltpu.SemaphoreType.DMA((RING_SLOTS,)),
        ],
    )
    out = pl.pallas_call(
        functools.partial(_dsa_sample_kernel, t_new, n_pages, kk),
        grid_spec=grid_spec,
        out_shape=jax.ShapeDtypeStruct((bd, TPAD, A_Q), BF16),
        compiler_params=_params("arbitrary"),
    )(page_table, qbd, qir, wr, knew, vnew, kinew, cache_kit, cache_kt, cache_vt)
    return out[:, :t_new].reshape(bd * t_new, A_Q)


def _glr_kernel(t_valid, layer, bg_ref, lbl_ref, gw_ref, s0_ref, o_ref, s_ref, st_ref):
    tt = bg_ref.shape[0]
    nb = tt // GLR_BLOCK
    it = pl.program_id(1)

    @pl.when(it == 0)
    def _():
        for hh in range(B_HEADS):
            st_ref[hh] = s0_ref[hh].astype(F32).T

    logits = lbl_ref[...]
    ex = jnp.exp(logits - jnp.max(logits, axis=0, keepdims=True))
    lb_all = jnp.sum(ex[:layer + 1], axis=0, keepdims=True) / jnp.sum(ex, axis=0, keepdims=True)

    row = lax.broadcasted_iota(jnp.int32, (tt, 1), 0)
    valid = (it * tt + row) < t_valid
    rloc = row % GLR_BLOCK
    t_idx = lax.broadcasted_iota(jnp.int32, (1, GLR_BLOCK, 1), 1)
    blk_of_col = lax.broadcasted_iota(jnp.int32, (nb, 1, tt), 2) // GLR_BLOCK
    blk_id = lax.broadcasted_iota(jnp.int32, (nb, 1, tt), 0)
    in_block = blk_of_col == blk_id
    gw = gw_ref[...]

    for hh in range(B_HEADS):
        lb = lb_all[:, B_KEY_DIM * hh:B_KEY_DIM * (hh + 1)]
        bq = bg_ref[:, B_KEY_DIM * hh:B_KEY_DIM * (hh + 1)]
        bf = bg_ref[:, B_K + B_KEY_DIM * hh:B_K + B_KEY_DIM * (hh + 1)]
        bi = bg_ref[:, 2 * B_K + B_VAL_DIM * hh:2 * B_K + B_VAL_DIM * (hh + 1)]
        bo = bg_ref[:, 2 * B_K + B_V + B_VAL_DIM * hh:2 * B_K + B_V + B_VAL_DIM * (hh + 1)]
        f = lb + (1.0 - lb) * jax.nn.sigmoid(bf)
        kx = jnp.where(valid, (1.0 - lb) * jax.nn.sigmoid(-bf), 0.0)
        qx = bq * jax.nn.sigmoid(bq)
        cum = jnp.where(valid, jnp.log(f), 0.0)
        for sh in (1, 2, 4, 8):
            cum = cum + jnp.where(rloc >= sh, pltpu.roll(cum, sh, 0), 0.0)
        b3 = cum.reshape(nb, GLR_BLOCK, B_KEY_DIM)
        q3 = qx.reshape(nb, GLR_BLOCK, B_KEY_DIM)
        k3 = kx.reshape(nb, GLR_BLOCK, B_KEY_DIM)
        v3 = bi.reshape(nb, GLR_BLOCK, B_VAL_DIM)
        b_end = b3[:, GLR_BLOCK - 1:GLR_BLOCK, :]
        q_dec = (q3 * jnp.exp(b3)).astype(BF16)
        k_dec = (k3 * jnp.exp(b_end - b3)).reshape(tt, B_KEY_DIM).astype(BF16)

        o_tiles = []
        for r0 in range(0, GLR_BLOCK, SUBLANES):
            rows = slice(r0, r0 + SUBLANES)
            b_t, q_t = b3[:, rows, :], q3[:, rows, :]
            t_pos = r0 + t_idx[:, :SUBLANES, :]
            o_t = jnp.zeros((nb, SUBLANES, B_VAL_DIM), F32)
            for s in range(min(GLR_BLOCK, r0 + SUBLANES)):
                gap = b_t - b3[:, s:s + 1, :]
                dec = jnp.exp(gap if s <= r0 else jnp.where(t_pos >= s, gap, NEG_INF))
                a = jnp.sum(q_t * k3[:, s:s + 1, :] * dec, axis=-1, keepdims=True)
                o_t = o_t + a * v3[:, s:s + 1, :]
            o_tiles.append(o_t)
        o_in = jnp.concatenate(o_tiles, axis=1)

        v_t = bi.T
        lhs = jnp.where(in_block, v_t[None], 0.0).astype(BF16).reshape(nb * B_VAL_DIM, tt)
        u_t = _dot(lhs, k_dec).reshape(nb, B_VAL_DIM, B_KEY_DIM)
        decay_end = jnp.exp(b_end)
        st = st_ref[hh]
        outs = []
        for jb in range(nb):
            outs.append(_dot_nt(q_dec[jb], st.astype(BF16)))
            st = st * decay_end[jb] + u_t[jb]
        st_ref[hh] = st
        o = jnp.concatenate(outs, axis=0) + o_in.reshape(tt, B_VAL_DIM)
        o = _rms(o, gw) * (bo * jax.nn.sigmoid(bo))
        o_ref[:, B_VAL_DIM * hh:B_VAL_DIM * (hh + 1)] = o.astype(o_ref.dtype)

    @pl.when(it == pl.num_programs(1) - 1)
    def _():
        for hh in range(B_HEADS):
            s_ref[hh] = st_ref[hh].T.astype(s_ref.dtype)


def _glr(bg4, lb_logits, glr_norm_w, s0, t_valid, tt, layer):
    b, t, _ = bg4.shape
    state = pl.BlockSpec((None, B_HEADS, B_KEY_DIM, B_VAL_DIM), lambda bi, i: (bi, 0, 0, 0))
    return pl.pallas_call(
        functools.partial(_glr_kernel, t_valid, layer),
        grid=(b, t // tt),
        in_specs=[pl.BlockSpec((None, tt, IN_GROUP_B), lambda bi, i: (bi, i, 0)),
                  pl.BlockSpec(lb_logits.shape, lambda bi, i: (0, 0)),
                  pl.BlockSpec((1, B_VAL_DIM), lambda bi, i: (0, 0)),
                  state],
        out_specs=(pl.BlockSpec((None, tt, B_V), lambda bi, i: (bi, i, 0)), state),
        out_shape=(jax.ShapeDtypeStruct((b, t, B_V), BF16), jax.ShapeDtypeStruct(s0.shape, s0.dtype)),
        scratch_shapes=[pltpu.VMEM((B_HEADS, B_VAL_DIM, B_KEY_DIM), F32)],
        compiler_params=_params("parallel", "arbitrary"),
    )(bg4, lb_logits, glr_norm_w, s0)


def _pool_kernel(n_hist, x_ref, prev_ref, g_ref, wp_ref, sc_ref, o_ref, hist_ref, carry_ref):
    tt, d = x_ref.shape
    group = d // len(POOL_WINDOWS)
    it = pl.program_id(1)

    @pl.when(it == 0)
    def _():
        carry_ref[...] = prev_ref[...]

    x = x_ref[...]
    h = _rms(x, g_ref[...])
    ext = jnp.concatenate([carry_ref[...], h], axis=0)
    pos = it * tt + lax.broadcasted_iota(jnp.int32, (tt, 1), 0) + n_hist
    run = ext
    width = 1
    for gi, win in enumerate(POOL_WINDOWS):
        while width < win:
            run = run + pltpu.roll(run, width, 0)
            width *= 2
        sl = slice(group * gi, group * (gi + 1))
        cnt = jnp.minimum(pos + 1, win).astype(F32)
        z = run[POOL_HALO:, sl] / cnt - h[:, sl]
        y = _dot(z.astype(BF16), wp_ref[gi]) * sc_ref[:, sl]
        o_ref[:, sl] = x[:, sl] + y
    tail = ext[tt:, :]
    carry_ref[...] = tail
    hist_ref[...] = tail


def _pool(x, prev, g, w_pool, scale, n_hist, tt):
    b, t, d = x.shape
    hist_spec = pl.BlockSpec((None, POOL_HALO, d), lambda bi, i: (bi, 0, 0))
    return pl.pallas_call(
        functools.partial(_pool_kernel, n_hist),
        grid=(b, t // tt),
        in_specs=[pl.BlockSpec((None, tt, d), lambda bi, i: (bi, i, 0)), hist_spec,
                  pl.BlockSpec((1, d), lambda bi, i: (0, 0)),
                  pl.BlockSpec(w_pool.shape, lambda bi, i: (0, 0, 0)),
                  pl.BlockSpec((1, d), lambda bi, i: (0, 0))],
        out_specs=(pl.BlockSpec((None, tt, d), lambda bi, i: (bi, i, 0)), hist_spec),
        out_shape=(jax.ShapeDtypeStruct((b, t, d), F32), jax.ShapeDtypeStruct((b, POOL_HALO, d), F32)),
        scratch_shapes=[pltpu.VMEM((POOL_HALO, d), F32)],
        compiler_params=_params("parallel", "arbitrary"),
    )(x, prev, g, w_pool, scale)


def _ffn_ple_kernel(n_chunks, has_mix, final, *refs):
    if has_mix:
        att_ref, glr_ref, wa_ref, wb_ref = refs[:4]
        refs = refs[4:]
    x_ref, p_ref, gf_ref, wg_ref, wu_ref, wd_ref, gp_ref, wpg_ref, wpe_ref, gfin_ref, o_ref = refs
    x = x_ref[...]
    if has_mix:
        x = x + _dot(att_ref[...], wa_ref[...]) + _dot(glr_ref[...], wb_ref[...])
    h = _rms(x, gf_ref[...]).astype(BF16)
    d_ff = wg_ref.shape[1]
    cw = d_ff // n_chunks
    acc = None
    for c in range(n_chunks):
        sl = slice(cw * c, cw * (c + 1))
        gate = _dot(h, wg_ref[:, sl])
        up = _dot(h, wu_ref[:, sl])
        part = _dot((gate * jax.nn.sigmoid(gate) * up).astype(BF16), wd_ref[sl, :])
        acc = part if acc is None else acc + part
    x = x + acc
    gate = jax.nn.sigmoid(_dot(_rms(x, gp_ref[...]).astype(BF16), wpg_ref[...]))
    x = x + _dot(p_ref[...].astype(BF16), wpe_ref[...]) * gate
    if final:
        x = _rms(x, gfin_ref[...])
    o_ref[...] = x


def _ffn_ple(x, mix, p, g_ffn, w_gate, w_up, w_down, g_ple, w_ple_gate, w_ple, g_final, final, tm):
    n, d = x.shape
    d_ff = w_gate.shape[1]
    n_chunks = 2 if d_ff % (2 * LANES) == 0 else 1
    row = lambda width: pl.BlockSpec((tm, width), lambda i: (i, 0))
    args = [x, p, g_ffn, w_gate, w_up, w_down, g_ple, w_ple_gate, w_ple, g_final]
    specs = [row(d), row(p.shape[1])] + [_const_spec(a.shape) for a in args[2:]]
    if mix is not None:
        args = list(mix) + args
        specs = [row(mix[0].shape[1]), row(mix[1].shape[1]), _const_spec(mix[2].shape), _const_spec(mix[3].shape)] + specs
    return pl.pallas_call(
        functools.partial(_ffn_ple_kernel, n_chunks, mix is not None, final),
        grid=(n // tm,),
        in_specs=specs,
        out_specs=row(d),
        out_shape=jax.ShapeDtypeStruct((n, d), F32),
        compiler_params=_params("parallel"),
    )(*args)


def _rope_tables(pos):
    half = A_HEAD_DIM // 2
    inv = ROPE_THETA ** (-jnp.arange(half, dtype=F32) / half)
    ang = pos.astype(F32)[:, None] * inv[None, :]
    c, s = jnp.cos(ang), jnp.sin(ang)
    reps = LANES // A_HEAD_DIM
    return jnp.tile(jnp.concatenate([c, c], axis=1), (1, reps)), jnp.tile(jnp.concatenate([-s, s], axis=1), (1, reps))


def _pack_w_in(w):
    d = w.shape[0]
    n_a = 3 * A_Q + IDX_Q
    n_i = IDX_DIM + IDX_HEADS
    return jnp.concatenate([w[:, :n_a], w[:, n_a:n_a + n_i], jnp.zeros((d, IN_GROUP_I - n_i), w.dtype),
                            w[:, n_a + n_i:]], axis=1).astype(BF16)


def _row_tile(n, want):
    t = min(n, want)
    assert n % t == 0
    return t


def _trunk(x, p, pos, attend, glr_s0, glr_tile, pool_prev, pool_hist, W):
    b, t, d = x.shape
    n = b * t
    tm = _row_tile(n, 256)
    row2 = lambda a: a.reshape(1, -1)
    cos_tab, sin_tab = _rope_tables(pos)
    depth = W["mix_norm"].shape[0]
    xf = x.reshape(n, d)
    outs = {}
    for i in range(depth):
        jj = i // 2
        if i % 2 == 0:
            bs, ts = (b, t) if t % tm == 0 else (1, n)
            res = _in_proj(xf.reshape(bs, ts, d), row2(W["mix_norm"][i]), _pack_w_in(W["w_in"][jj]),
                           cos_tab, sin_tab, tm)
            flat = lambda a: a.reshape(n, a.shape[2])
            qb, kt, kb, vt, vb, vtb, qib, kiw, kit, wt, ki2, bg4 = res
            qb, kb, vb, qib, kiw, ki2, bg4 = [flat(a) for a in (qb, kb, vb, qib, kiw, ki2, bg4)]
            att = attend(jj, qb, qib, kiw, ki2, kb, vb, wt, vtb)
            t_pad = -(-t // glr_tile) * glr_tile
            bg3 = bg4.reshape(b, t, IN_GROUP_B)
            if t_pad != t:
                bg3 = jnp.pad(bg3, ((0, 0), (0, t_pad - t), (0, 0)))
            glr, s_new = _glr(bg3, W["lb_logits"], row2(W["glr_norm"][jj]), glr_s0[jj], t, glr_tile, jj)
            glr = glr[:, :t].reshape(n, B_V)
            w_out = W["w_out"][jj].astype(BF16)
            mix = (att, glr, w_out[:A_Q], w_out[A_Q:])
            heads = lambda a: a.reshape(bs, A_HEADS, A_HEAD_DIM, ts).transpose(0, 3, 1, 2).reshape(
                b, t, A_HEADS, A_HEAD_DIM)
            outs.setdefault("k", []).append(heads(kt))
            outs.setdefault("v", []).append(heads(vt))
            outs.setdefault("ki", []).append(kit.transpose(0, 2, 1).reshape(b, t, IDX_DIM))
            outs.setdefault("s", []).append(s_new)
        else:
            tt = _row_tile(t, 256)
            t_pad = -(-t // 8) * 8
            x3 = xf.reshape(b, t, d)
            if t_pad != t:
                x3 = jnp.pad(x3, ((0, 0), (0, t_pad - t), (0, 0)))
                tt = t_pad
            y3, hist = _pool(x3, pool_prev[jj], row2(W["mix_norm"][i]), W["pool_w"][jj].astype(BF16),
                             row2(W["pool_scale"][jj]), pool_hist, tt)
            xf = y3[:, :t].reshape(n, d)
            mix = None
            outs.setdefault("hist", []).append((hist, t_pad))
        d_ff = W["w_down"].shape[1]
        w_up = W["w_up"][i].astype(BF16)
        xf = _ffn_ple(xf, mix, p[i].reshape(n, -1), row2(W["ffn_norm"][i]), w_up[:, :d_ff], w_up[:, d_ff:],
                      W["w_down"][i].astype(BF16), row2(W["ple_norm"][i]), W["w_ple_gate"][i].astype(BF16),
                      W["w_ple"][i].astype(BF16), row2(W["final_norm"]), i == depth - 1, tm)
    return xf.reshape(b, t, d), outs


def kernel(x_prompt, x_sample, cache_k, cache_v, cache_kidx, state_glr, state_pool, page_table, p_prompt, p_sample,
           mix_norm, w_in, w_out, lb_logits, glr_norm, pool_w, pool_scale, ffn_norm, w_up, w_down, ple_norm,
           w_ple_gate, w_ple, final_norm):
    W = dict(mix_norm=mix_norm, w_in=w_in, w_out=w_out, lb_logits=lb_logits, glr_norm=glr_norm, pool_w=pool_w,
             pool_scale=pool_scale, ffn_norm=ffn_norm, w_up=w_up, w_down=w_down, ple_norm=ple_norm,
             w_ple_gate=w_ple_gate, w_ple=w_ple, final_norm=final_norm)
    n_ab = w_in.shape[0]
    n_c = pool_w.shape[0]
    d = x_prompt.shape[-1]
    n_buf = max(POOL_WINDOWS) - 1

    bp, sp = x_prompt.shape[:2]

    def attend_prompt(jj, qb, qib, kiw, ki2, kb, vb, wt, vtb):
        r3 = lambda a: a.reshape(bp, sp, a.shape[-1])
        return _dsa_prompt(r3(qb), r3(qib), wt, r3(ki2), r3(kb), vtb, vtb.shape[-1]).reshape(bp * sp, A_Q)

    glr0_p = [jnp.zeros((bp, B_HEADS, B_KEY_DIM, B_VAL_DIM), state_glr.dtype) for _ in range(n_ab)]
    pool0_p = [jnp.zeros((bp, POOL_HALO, d), F32) for _ in range(n_c)]
    yp, op = _trunk(x_prompt, p_prompt, jnp.arange(sp, dtype=jnp.int32), attend_prompt, glr0_p,
                    _row_tile(sp, 256), pool0_p, 0, W)

    bd, ts = x_sample.shape[:2]
    past = page_table.shape[1] * PAGE_SIZE

    n_pool = cache_k.shape[1]
    cache_kt = jnp.transpose(cache_k, (0, 1, 3, 4, 2)).reshape(n_ab, n_pool, A_Q, PAGE_SIZE)
    cache_vt = jnp.transpose(cache_v, (0, 1, 3, 4, 2)).reshape(n_ab, n_pool, A_Q, PAGE_SIZE)
    cache_kit = jnp.transpose(cache_kidx, (0, 1, 3, 2))

    def attend_sample(jj, qb, qib, kiw, ki2, kb, vb, wt, vtb):
        return _dsa_sample(qb, qib, kiw, ki2, kb, vb, cache_kt[jj], cache_vt[jj], cache_kit[jj], page_table, ts)

    pos_s = jnp.tile(past + jnp.arange(ts, dtype=jnp.int32), bd)
    pool0_s = [jnp.pad(state_pool[j].astype(F32), ((0, 0), (POOL_HALO - n_buf, 0), (0, 0))) for j in range(n_c)]
    ys, os_ = _trunk(x_sample, p_sample, pos_s, attend_sample, [state_glr[j] for j in range(n_ab)],
                     LANES, pool0_s, n_buf, W)

    def pool_rows(o, prev, t):
        res = []
        for j, (hist, t_pad) in enumerate(o["hist"]):
            if t >= n_buf:
                res.append(hist[:, POOL_HALO - (t_pad - t) - n_buf:POOL_HALO - (t_pad - t)])
            else:
                new = hist[:, POOL_HALO - t_pad:POOL_HALO - t_pad + t]
                res.append(jnp.concatenate([prev[j][:, t:].astype(new.dtype), new], axis=1))
        return jnp.stack(res)

    return (yp, ys, jnp.stack(op["k"]), jnp.stack(op["v"]), jnp.stack(op["ki"]), jnp.stack(op["s"]),
            pool_rows(op, None, sp).astype(x_prompt.dtype),
            jnp.stack(os_["k"]), jnp.stack(os_["v"]), jnp.stack(os_["ki"]), jnp.stack(os_["s"]),
            pool_rows(os_, state_pool, ts).astype(x_sample.dtype))
```

```python
import functools

import jax
import jax.numpy as jnp
from jax import lax
from jax.experimental import pallas as pl
from jax.experimental.pallas import tpu as pltpu

F32 = jnp.float32
BF16 = jnp.bfloat16

A_HEADS = 8
A_HEAD_DIM = 64
IDX_HEADS = 8
IDX_DIM = 64
TOPK_MAX = 256
ROPE_THETA = 10000.0
B_HEADS = 4
B_KEY_DIM = 128
B_VAL_DIM = 128
GLR_BLOCK = 16
POOL_WINDOWS = (2, 4, 8, 16)
POOL_HALO = 16
PAGE_SIZE = 128
RMS_EPS = 1e-6

A_Q = A_HEADS * A_HEAD_DIM
IDX_Q = IDX_HEADS * IDX_DIM
B_K = B_HEADS * B_KEY_DIM
B_V = B_HEADS * B_VAL_DIM

LANES = 128
VMEM_LIMIT = 56 * 1024 * 1024
INT_MIN = -(2 ** 31)
NEG_INF = float("-inf")


def _params(*sem):
    return pltpu.CompilerParams(dimension_semantics=sem, vmem_limit_bytes=VMEM_LIMIT)


def _rms(x, g):
    ms = jnp.mean(x * x, axis=-1, keepdims=True)
    return x * lax.rsqrt(ms + RMS_EPS) * g


def _dot(a, b):
    return jnp.dot(a, b, preferred_element_type=F32)


def _dot_nt(a, b):
    return lax.dot_general(a, b, (((1,), (1,)), ((), ())), preferred_element_type=F32)


def _const_spec(shape):
    nd = len(shape)
    return pl.BlockSpec(shape, lambda *_: (0,) * nd, pipeline_mode=pl.Buffered(1))


IN_GROUP_A = 4 * A_Q
IN_GROUP_I = LANES
IN_GROUP_B = 2 * B_K + 2 * B_V
IN_PACKED = IN_GROUP_A + IN_GROUP_I + IN_GROUP_B


def _in_proj_kernel(x_ref, g_ref, w_ref, cos_ref, sin_ref,
                    qb_ref, kt_ref, kb_ref, vt_ref, vb_ref, vtb_ref, qib_ref, kiw_ref, kit_ref, wt_ref, ki2_ref,
                    bg_ref):
    tm = x_ref.shape[0]
    h = _rms(x_ref[...], g_ref[...]).astype(BF16)
    cos = cos_ref[...]
    sin = sin_ref[...]
    lane = lax.broadcasted_iota(jnp.int32, (tm, LANES), 1)
    first = (lane % A_HEAD_DIM) < (A_HEAD_DIM // 2)
    slab = lambda i: slice(LANES * i, LANES * (i + 1))

    def rope(z, c, s):
        rot = jnp.where(first, pltpu.roll(z, LANES - A_HEAD_DIM // 2, 1), pltpu.roll(z, A_HEAD_DIM // 2, 1))
        return z * c + rot * s

    def roped(col0):
        z = _dot(h, w_ref[:, col0:col0 + A_Q])
        return [rope(z[:, slab(i)], cos, sin) for i in range(A_Q // LANES)]

    scale = A_HEAD_DIM ** -0.5
    for i, r in enumerate(roped(0)):
        qb_ref[:, slab(i)] = (r * scale).astype(BF16)
    for i, r in enumerate(roped(A_Q)):
        kt_ref[slab(i), :] = r.T
        kb_ref[:, slab(i)] = r.astype(BF16)
    v = _dot(h, w_ref[:, 2 * A_Q:3 * A_Q])
    for i in range(A_Q // LANES):
        v_t = v[:, slab(i)].T
        vt_ref[slab(i), :] = v_t
        vtb_ref[slab(i), :] = v_t.astype(BF16)
    vb_ref[...] = v.astype(BF16)
    for i, r in enumerate(roped(3 * A_Q)):
        qib_ref[:, slab(i)] = (r * (IDX_DIM ** -0.5)).astype(BF16)
    is_key = lane < IDX_DIM
    z = _dot(h, w_ref[:, IN_GROUP_A:IN_GROUP_A + IN_GROUP_I])
    kiw = rope(z, jnp.where(is_key, cos, 1.0), jnp.where(is_key, sin, 0.0))
    kiw_ref[...] = kiw
    kiw_t = kiw.T
    kit_ref[...] = kiw_t[:IDX_DIM, :]
    wt_ref[...] = kiw_t[IDX_DIM:IDX_DIM + IDX_HEADS, :]
    ki2_ref[...] = jnp.where(is_key, kiw, pltpu.roll(kiw, IDX_DIM, 1)).astype(BF16)
    col0 = IN_GROUP_A + IN_GROUP_I
    for i in range(IN_GROUP_B // A_Q):
        bg_ref[:, A_Q * i:A_Q * (i + 1)] = _dot(h, w_ref[:, col0 + A_Q * i:col0 + A_Q * (i + 1)])


def _in_proj(x, g, w_packed, cos_tab, sin_tab, tm):
    b, t, d = x.shape
    row = lambda width: pl.BlockSpec((None, tm, width), lambda bi, i: (bi, i, 0))
    col = lambda height: pl.BlockSpec((None, height, tm), lambda bi, i: (bi, 0, i))
    tab = pl.BlockSpec((tm, LANES), lambda bi, i: (i, 0))
    rows = lambda width, dt: (jax.ShapeDtypeStruct((b, t, width), dt), row(width))
    cols = lambda height: (jax.ShapeDtypeStruct((b, height, t), F32), col(height))
    outs = (
        rows(A_Q, BF16),
        cols(A_Q),
        rows(A_Q, BF16),
        cols(A_Q),
        rows(A_Q, BF16),
        (jax.ShapeDtypeStruct((b, t // tm, A_Q, tm), BF16),
         pl.BlockSpec((None, None, A_Q, tm), lambda bi, i: (bi, i, 0, 0))),
        rows(IDX_Q, BF16),
        rows(LANES, F32),
        cols(IDX_DIM),
        cols(IDX_HEADS),
        rows(LANES, BF16),
        rows(IN_GROUP_B, F32),
    )
    return pl.pallas_call(
        _in_proj_kernel,
        grid=(b, t // tm),
        in_specs=[row(d), _const_spec((1, d)), _const_spec((d, IN_PACKED)), tab, tab],
        out_specs=tuple(o[1] for o in outs),
        out_shape=tuple(o[0] for o in outs),
        compiler_params=_params("parallel", "parallel"),
    )(x, g, w_packed, cos_tab, sin_tab)


def _level_of(code):
    signed = code ^ jnp.int32(INT_MIN)
    return pltpu.bitcast(signed ^ ((signed >> 31) & jnp.int32(0x7FFFFFFF)), F32)


def _unrolled(n, body, init):
    for c in range(n):
        init = body(c, init)
    return init


def _fold_lanes(t):
    acc = t[:, :LANES]
    for i in range(1, t.shape[1] // LANES):
        acc = acc + t[:, LANES * i:LANES * (i + 1)]
    return acc


def _select_bias(n_c, width, get_key, get_bias, set_bias, kk, use_topk):
    rows = use_topk.shape[0]
    kf = float(kk)
    zeros = jnp.zeros((rows, LANES), F32)

    def count(cmp, level):
        level_b = jnp.broadcast_to(level, (rows, LANES))
        parts = [zeros] * 4
        for c in range(n_c):
            k = get_key(c)
            for i in range(width // LANES):
                hit = jnp.where(cmp(k[:, LANES * i:LANES * (i + 1)], level_b), 1.0, 0.0)
                parts[(c + i) % 4] = parts[(c + i) % 4] + hit
        return jnp.sum((parts[0] + parts[1]) + (parts[2] + parts[3]), axis=1, keepdims=True)

    def bisect(i, t):
        cand = t | jnp.left_shift(jnp.int32(1), 31 - i)
        return jnp.where(count(jnp.greater_equal, _level_of(cand)) >= kf, cand, t)

    thr = _level_of(lax.fori_loop(0, 32, bisect, jnp.zeros((rows, 1), jnp.int32)))

    def apply(c, acc):
        ge = get_key(c) >= thr
        set_bias(c, jnp.where(use_topk, jnp.where(ge, 0.0, NEG_INF), get_bias(c)))
        return acc + _fold_lanes(jnp.where(ge, 1.0, 0.0))

    cnt_ge = jnp.sum(_unrolled(n_c, apply, zeros), axis=1, keepdims=True)
    tied = jnp.logical_and(use_topk, cnt_ge != kf)

    @pl.when(jnp.max(jnp.where(tied, 1.0, 0.0)) > 0.0)
    def _():
        room = kf - count(jnp.greater, thr)
        before = (lax.broadcasted_iota(jnp.int32, (width, width), 0)
                  < lax.broadcasted_iota(jnp.int32, (width, width), 1))
        tri = jnp.where(before, 1.0, 0.0).astype(BF16)

        def fix(c, seen):
            kc = get_key(c)
            eq = jnp.where(kc == thr, 1.0, 0.0)
            rank = _dot(eq.astype(BF16), tri) + seen
            keep = jnp.logical_or(kc > thr, jnp.logical_and(kc == thr, rank < room))
            set_bias(c, jnp.where(use_topk, jnp.where(keep, 0.0, NEG_INF), get_bias(c)))
            return seen + jnp.sum(eq, axis=1, keepdims=True)

        _unrolled(n_c, fix, jnp.zeros((rows, 1), F32))


def _pair_masks(rows):
    lane = lax.broadcasted_iota(jnp.int32, (rows, LANES), 1)
    lo = lane < A_HEAD_DIM
    return lo, jnp.logical_not(lo)


M_INIT = -1e30


SUBLANES = 8
LOG2_E = 1.4426950408889634


def _fold_rows(a, op):
    return op(a.reshape(a.shape[0] // SUBLANES, SUBLANES, a.shape[1]), axis=0)


def _select_bias_t(n_c, key_ref, bias_ref, kk, use_topk):
    kc, rows = key_ref.shape[1:]
    kf = float(kk)
    zeros = jnp.zeros((SUBLANES, rows), F32)

    def count(cmp, level):
        level8 = jnp.broadcast_to(level, (SUBLANES, rows))[None, None]

        def body(c, a):
            k4 = key_ref[c].reshape(4, kc // (4 * SUBLANES), SUBLANES, rows)
            part = jnp.sum(jnp.where(cmp(k4, level8), 1.0, 0.0), axis=1)
            return a + ((part[0] + part[1]) + (part[2] + part[3]))

        return jnp.sum(lax.fori_loop(0, n_c, body, zeros), axis=0, keepdims=True)

    def bisect(i, t):
        cand = t | jnp.left_shift(jnp.int32(1), 31 - i)
        return jnp.where(count(jnp.greater_equal, _level_of(cand)) >= kf, cand, t)

    thr = _level_of(lax.fori_loop(0, 32, bisect, jnp.zeros((1, rows), jnp.int32)))

    def apply(c, acc):
        ge = key_ref[c] >= thr
        bias_ref[c] = jnp.where(use_topk, jnp.where(ge, 0.0, NEG_INF), bias_ref[c])
        return acc + _fold_rows(jnp.where(ge, 1.0, 0.0), jnp.sum)

    cnt_ge = jnp.sum(lax.fori_loop(0, n_c, apply, zeros), axis=0, keepdims=True)
    tied = jnp.logical_and(use_topk, cnt_ge != kf)

    @pl.when(jnp.max(jnp.where(tied, 1.0, 0.0)) > 0.0)
    def _():
        room = kf - count(jnp.greater, thr)
        earlier = (lax.broadcasted_iota(jnp.int32, (kc, kc), 1) < lax.broadcasted_iota(jnp.int32, (kc, kc), 0))
        tri = jnp.where(earlier, 1.0, 0.0).astype(BF16)

        def fix(c, seen):
            k = key_ref[c]
            eq = jnp.where(k == thr, 1.0, 0.0)
            rank = _dot(tri, eq.astype(BF16)) + seen
            keep = jnp.logical_or(k > thr, jnp.logical_and(k == thr, rank < room))
            bias_ref[c] = jnp.where(use_topk, jnp.where(keep, 0.0, NEG_INF), bias_ref[c])
            return seen + jnp.sum(eq, axis=0, keepdims=True)

        lax.fori_loop(0, n_c, fix, jnp.zeros((1, rows), F32))


def _dsa_prompt_kernel(kk, q_ref, qi_ref, wt_ref, ki2_ref, k_ref, vt_ref, o_ref,
                       key_ref, bias_ref, qh_ref, s_ref, max_ref, sum_ref, acc_ref):
    tq = q_ref.shape[0]
    kc = key_ref.shape[1]
    j = pl.program_id(1)
    n_c = j + 1
    qpos = j * tq + lax.broadcasted_iota(jnp.int32, (1, tq), 1)
    krow = lax.broadcasted_iota(jnp.int32, (kc, 1), 0)
    masks = _pair_masks(tq)
    slab = lambda p: slice(LANES * p, LANES * (p + 1))
    head_rows = lambda hh: slice(A_HEAD_DIM * hh, A_HEAD_DIM * (hh + 1))
    keys_of = lambda c: pl.ds(pl.multiple_of(c * kc, kc), kc)
    one_head = lambda ref, hh: jnp.where(masks[hh % 2], ref[:, slab(hh // 2)], jnp.zeros((), ref.dtype))
    causal = lambda c: c * kc + krow <= qpos

    def init_bias(c, _):
        bias_ref[c] = jnp.where(causal(c), 0.0, NEG_INF)
        return 0

    lax.fori_loop(0, n_c, init_bias, 0)

    @pl.when(n_c * kc > kk)
    def _():
        w = wt_ref[...] * (IDX_HEADS ** -0.5)
        for hh in range(IDX_HEADS):
            qh_ref[hh] = one_head(qi_ref, hh)

        def score_chunk(c, _):
            ki2 = ki2_ref[keys_of(c), :]
            acc = None
            for hh in range(IDX_HEADS):
                term = jnp.maximum(_dot_nt(ki2, qh_ref[hh]), 0.0) * w[hh:hh + 1, :]
                acc = term if acc is None else acc + term
            key_ref[c] = jnp.where(causal(c), acc, NEG_INF)
            return 0

        lax.fori_loop(0, n_c, score_chunk, 0)
        _select_bias_t(n_c, key_ref, bias_ref, kk, qpos >= kk)

    for hh in range(A_HEADS):
        qh_ref[hh] = one_head(q_ref, hh)
        max_ref[hh] = jnp.full((SUBLANES, tq), M_INIT, F32)
        sum_ref[hh] = jnp.zeros((SUBLANES, tq), F32)
    acc_ref[...] = jnp.zeros(acc_ref.shape, F32)

    def max_chunk(c, _):
        bias = bias_ref[c]
        for hh in range(A_HEADS):
            s = _dot_nt(k_ref[keys_of(c), slab(hh // 2)], qh_ref[hh]) * LOG2_E + bias
            s_ref[hh, c] = s
            max_ref[hh] = jnp.maximum(max_ref[hh], _fold_rows(s, jnp.max))
        return 0

    lax.fori_loop(0, n_c, max_chunk, 0)
    for hh in range(A_HEADS):
        max_ref[hh] = jnp.broadcast_to(jnp.max(max_ref[hh], axis=0, keepdims=True), (SUBLANES, tq))

    def sum_chunk(c, _):
        for hh in range(A_HEADS):
            e = jnp.exp2(s_ref[hh, c].reshape(kc // SUBLANES, SUBLANES, tq) - max_ref[hh][None])
            sum_ref[hh] = sum_ref[hh] + jnp.sum(e, axis=0)
            pv = _dot(vt_ref[c, head_rows(hh), :], e.reshape(kc, tq).astype(BF16))
            acc_ref[head_rows(hh), :] = acc_ref[head_rows(hh), :] + pv
        return 0

    lax.fori_loop(0, n_c, sum_chunk, 0)
    for p in range(A_Q // LANES):
        parts = [acc_ref[head_rows(hh), :] / jnp.sum(sum_ref[hh], axis=0, keepdims=True) for hh in (2 * p, 2 * p + 1)]
        o_ref[:, slab(p)] = jnp.concatenate(parts, axis=0).T.astype(o_ref.dtype)


def _dsa_prompt(qb, qib, wt, ki2, kb, vtb, tq):
    b, s, _ = qb.shape
    n_c = s // tq
    assert vtb.shape == (b, n_c, A_Q, tq)
    kk = min(TOPK_MAX, s // 4)
    tile = lambda width: pl.BlockSpec((None, tq, width), lambda bi, j: (bi, j, 0))
    full = lambda width: pl.BlockSpec((None, s, width), lambda bi, j: (bi, 0, 0))
    return pl.pallas_call(
        functools.partial(_dsa_prompt_kernel, kk),
        grid=(b, n_c),
        in_specs=[tile(A_Q), tile(IDX_Q), pl.BlockSpec((None, IDX_HEADS, tq), lambda bi, j: (bi, 0, j)),
                  full(LANES), full(A_Q), pl.BlockSpec((None, n_c, A_Q, tq), lambda bi, j: (bi, 0, 0, 0))],
        out_specs=tile(A_Q),
        out_shape=jax.ShapeDtypeStruct((b, s, A_Q), BF16),
        scratch_shapes=[pltpu.VMEM((n_c, tq, tq), F32), pltpu.VMEM((n_c, tq, tq), F32),
                        pltpu.VMEM((A_HEADS, tq, LANES), BF16), pltpu.VMEM((A_HEADS, n_c, tq, tq), F32),
                        pltpu.VMEM((A_HEADS, SUBLANES, tq), F32), pltpu.VMEM((A_HEADS, SUBLANES, tq), F32),
                        pltpu.VMEM((A_Q, tq), F32)],
        compiler_params=_params("parallel", "arbitrary"),
    )(qb, qib, wt, ki2, kb, vtb)


TPAD = 8
NEW_PAD = LANES
CHUNK_PAGES = 16
PAGE_UNROLL = 8
RING_SLOTS = 6


def _dsa_sample_kernel(t_new, n_pages, kk, pt_ref, qbd_ref, qir_ref, wr_ref, knew_ref, vnew_ref, kinew_ref,
                       ckidx_ref, ck_ref, cv_ref, o_ref,
                       kibuf, kvbuf, logit_ref, key_ref, bias_ref, sem_ki, sem_kv):
    b = pl.program_id(0)
    past = n_pages * PAGE_SIZE
    rows = A_HEADS * TPAD
    n_chunks = n_pages // CHUNK_PAGES

    def ki_copy(p):
        return pltpu.make_async_copy(ckidx_ref.at[pt_ref[b, p]], kibuf.at[p], sem_ki)

    def kv_copy(src_ref, c, i, slot):
        return pltpu.make_async_copy(src_ref.at[pt_ref[b, c * CHUNK_PAGES + i]], kvbuf.at[slot, i], sem_kv.at[slot])

    def start_chunk(src_ref, c, slot):
        lax.fori_loop(0, CHUNK_PAGES, lambda i, _: (kv_copy(src_ref, c, i, slot).start(), 0)[1], 0)

    def wait_chunk(src_ref, c, slot):
        lax.fori_loop(0, CHUNK_PAGES, lambda i, _: (kv_copy(src_ref, c, i, slot).wait(), 0)[1], 0)

    n_xfer = 2 * n_chunks

    def xfer(t):
        return (ck_ref, t, t % RING_SLOTS) if t < n_chunks else (cv_ref, t - n_chunks, t % RING_SLOTS)

    lax.fori_loop(0, n_pages, lambda p, _: (ki_copy(p).start(), 0)[1], 0)
    for t in range(min(RING_SLOTS, n_xfer)):
        start_chunk(*xfer(t))
    lax.fori_loop(0, n_pages, lambda p, _: (ki_copy(p).wait(), 0)[1], 0)

    qir = qir_ref[...]
    w = wr_ref[...] * (IDX_HEADS ** -0.5)
    tok = lax.broadcasted_iota(jnp.int32, (TPAD, 1), 0)
    tok_ok = tok < t_new

    def keys_of(s, ok):
        t = jnp.maximum(s, 0.0) * w
        return jnp.where(ok, jnp.sum(t.reshape(IDX_HEADS, TPAD, LANES), axis=0), NEG_INF)

    def index_page(p, _):
        key_ref[p] = keys_of(_dot(qir, kibuf[p].astype(BF16)), tok_ok)
        return 0

    lax.fori_loop(0, n_pages, index_page, 0, unroll=PAGE_UNROLL)
    new_idx = lax.broadcasted_iota(jnp.int32, (1, NEW_PAD), 1)
    new_ok = jnp.logical_and(jnp.logical_and(new_idx <= tok, new_idx < t_new), tok_ok)
    key_ref[n_pages] = keys_of(_dot_nt(qir, kinew_ref[...]), new_ok)
    bias_ref[0:n_pages] = jnp.broadcast_to(jnp.where(tok_ok, 0.0, NEG_INF)[None], (n_pages, TPAD, LANES))
    bias_ref[n_pages] = jnp.where(new_ok, 0.0, NEG_INF)

    def set_bias(c, val):
        bias_ref[c] = val

    use_topk = jnp.logical_and(past + tok + 1 > kk, tok_ok)
    _select_bias(n_pages + 1, LANES, lambda c: key_ref[c], lambda c: bias_ref[c], set_bias, kk, use_topk)

    qbd = qbd_ref[...]
    for c in range(n_chunks):
        slot = c % RING_SLOTS
        wait_chunk(*xfer(c))

        def logits_page(i, _, c=c, slot=slot):
            logit_ref[c * CHUNK_PAGES + i] = _dot(qbd, kvbuf[slot, i].astype(BF16))
            return 0

        lax.fori_loop(0, CHUNK_PAGES, logits_page, 0, unroll=PAGE_UNROLL)
        if c + RING_SLOTS < n_xfer:
            start_chunk(*xfer(c + RING_SLOTS))
    logit_ref[n_pages] = _dot_nt(qbd, knew_ref[...])
    s = logit_ref[...].reshape(n_pages + 1, A_HEADS, TPAD, LANES) + bias_ref[...][:, None]
    s = jnp.where(tok_ok[None, None], s, 0.0).reshape(n_pages + 1, rows, LANES)
    m = jnp.max(jnp.max(s, axis=0), axis=1, keepdims=True)
    e = jnp.exp(s - m[None])
    l = jnp.sum(jnp.sum(e, axis=0), axis=1, keepdims=True)
    logit_ref[...] = e

    acc = _dot(logit_ref[n_pages].astype(BF16), vnew_ref[...])
    for c in range(n_chunks):
        slot = (n_chunks + c) % RING_SLOTS
        wait_chunk(*xfer(n_chunks + c))

        def value_page(i, a, c=c, slot=slot):
            return a + _dot_nt(logit_ref[c * CHUNK_PAGES + i].astype(BF16), kvbuf[slot, i].astype(BF16))

        acc = lax.fori_loop(0, CHUNK_PAGES, value_page, acc, unroll=PAGE_UNROLL)
        if n_chunks + c + RING_SLOTS < n_xfer:
            start_chunk(*xfer(n_chunks + c + RING_SLOTS))
    acc = (acc / l).reshape(A_HEADS, TPAD, A_Q)
    head_of_lane = lax.broadcasted_iota(jnp.int32, (TPAD, A_Q), 1) // A_HEAD_DIM
    out = jnp.zeros((TPAD, A_Q), F32)
    for hh in range(A_HEADS):
        out = out + jnp.where(head_of_lane == hh, acc[hh], 0.0)
    o_ref[...] = out.astype(o_ref.dtype)


def _dsa_sample(qb, qib, kiw, ki2, kb, vb, cache_kt, cache_vt, cache_kit, page_table, t_new):
    bd, n_pages = page_table.shape
    assert n_pages % CHUNK_PAGES == 0 and t_new <= TPAD
    past = n_pages * PAGE_SIZE
    kk = min(TOPK_MAX, (past + t_new) // 4)
    rows = A_HEADS * TPAD

    def pad_tokens(a, to):
        a = a.reshape(bd, t_new, a.shape[-1])
        return jnp.pad(a, ((0, 0), (0, to - t_new), (0, 0)))

    q8 = pad_tokens(qb, TPAD)
    head_of_lane = jnp.arange(A_Q) // A_HEAD_DIM
    qbd = jnp.where(head_of_lane[None, None, None, :] == jnp.arange(A_HEADS)[None, :, None, None],
                    q8[:, None], jnp.zeros((), BF16)).reshape(bd, rows, A_Q)
    qir = pad_tokens(qib, TPAD).reshape(bd, TPAD, IDX_HEADS, IDX_DIM).transpose(0, 2, 1, 3).reshape(bd, rows, IDX_DIM)
    wr = pad_tokens(kiw[:, IDX_DIM:IDX_DIM + IDX_HEADS], TPAD).transpose(0, 2, 1).reshape(bd, rows, 1)
    knew = pad_tokens(kb, NEW_PAD)
    vnew = pad_tokens(vb, NEW_PAD)
    kinew = pad_tokens(ki2[:, :IDX_DIM], NEW_PAD)

    per_seq = lambda r, c: pl.BlockSpec((None, r, c), lambda b, pt: (b, 0, 0))
    any_spec = pl.BlockSpec(memory_space=pl.ANY)
    grid_spec = pltpu.PrefetchScalarGridSpec(
        num_scalar_prefetch=1,
        grid=(bd,),
        in_specs=[per_seq(rows, A_Q), per_seq(rows, IDX_DIM), per_seq(rows, 1), per_seq(NEW_PAD, A_Q),
                  per_seq(NEW_PAD, A_Q), per_seq(NEW_PAD, IDX_DIM), any_spec, any_spec, any_spec],
        out_specs=per_seq(TPAD, A_Q),
        scratch_shapes=[
            pltpu.VMEM((n_pages, IDX_DIM, PAGE_SIZE), F32),
            pltpu.VMEM((RING_SLOTS, CHUNK_PAGES, A_Q, PAGE_SIZE), F32),
            pltpu.VMEM((n_pages + 1, rows, LANES), F32),
            pltpu.VMEM((n_pages + 1, TPAD, LANES), F32),
            pltpu.VMEM((n_pages + 1, TPAD, LANES), F32),
            pltpu.SemaphoreType.DMA(()),
            pltpu.SemaphoreType.DMA((RING_SLOTS,)),
        ],
    )
    out = pl.pallas_call(
        functools.partial(_dsa_sample_kernel, t_new, n_pages, kk),
        grid_spec=grid_spec,
        out_shape=jax.ShapeDtypeStruct((bd, TPAD, A_Q), BF16),
        compiler_params=_params("arbitrary"),
    )(page_table, qbd, qir, wr, knew, vnew, kinew, cache_kit, cache_kt, cache_vt)
    return out[:, :t_new].reshape(bd * t_new, A_Q)


def _glr_kernel(t_valid, layer, bg_ref, lbl_ref, gw_ref, s0_ref, o_ref, s_ref, st_ref):
    tt = bg_ref.shape[0]
    nb = tt // GLR_BLOCK
    it = pl.program_id(1)

    @pl.when(it == 0)
    def _():
        for hh in range(B_HEADS):
            st_ref[hh] = s0_ref[hh].astype(F32).T

    logits = lbl_ref[...]
    ex = jnp.exp(logits - jnp.max(logits, axis=0, keepdims=True))
    lb_all = jnp.sum(ex[:layer + 1], axis=0, keepdims=True) / jnp.sum(ex, axis=0, keepdims=True)

    row = lax.broadcasted_iota(jnp.int32, (tt, 1), 0)
    valid = (it * tt + row) < t_valid
    rloc = row % GLR_BLOCK
    t_idx = lax.broadcasted_iota(jnp.int32, (1, GLR_BLOCK, 1), 1)
    blk_of_col = lax.broadcasted_iota(jnp.int32, (nb, 1, tt), 2) // GLR_BLOCK
    blk_id = lax.broadcasted_iota(jnp.int32, (nb, 1, tt), 0)
    in_block = blk_of_col == blk_id
    gw = gw_ref[...]

    for hh in range(B_HEADS):
        lb = lb_all[:, B_KEY_DIM * hh:B_KEY_DIM * (hh + 1)]
        bq = bg_ref[:, B_KEY_DIM * hh:B_KEY_DIM * (hh + 1)]
        bf = bg_ref[:, B_K + B_KEY_DIM * hh:B_K + B_KEY_DIM * (hh + 1)]
        bi = bg_ref[:, 2 * B_K + B_VAL_DIM * hh:2 * B_K + B_VAL_DIM * (hh + 1)]
        bo = bg_ref[:, 2 * B_K + B_V + B_VAL_DIM * hh:2 * B_K + B_V + B_VAL_DIM * (hh + 1)]
        f = lb + (1.0 - lb) * jax.nn.sigmoid(bf)
        kx = jnp.where(valid, (1.0 - lb) * jax.nn.sigmoid(-bf), 0.0)
        qx = bq * jax.nn.sigmoid(bq)
        cum = jnp.where(valid, jnp.log(f), 0.0)
        for sh in (1, 2, 4, 8):
            cum = cum + jnp.where(rloc >= sh, pltpu.roll(cum, sh, 0), 0.0)
        b3 = cum.reshape(nb, GLR_BLOCK, B_KEY_DIM)
        q3 = qx.reshape(nb, GLR_BLOCK, B_KEY_DIM)
        k3 = kx.reshape(nb, GLR_BLOCK, B_KEY_DIM)
        v3 = bi.reshape(nb, GLR_BLOCK, B_VAL_DIM)
        b_end = b3[:, GLR_BLOCK - 1:GLR_BLOCK, :]
        q_dec = (q3 * jnp.exp(b3)).astype(BF16)
        k_dec = (k3 * jnp.exp(b_end - b3)).reshape(tt, B_KEY_DIM).astype(BF16)

        o_tiles = []
        for r0 in range(0, GLR_BLOCK, SUBLANES):
            rows = slice(r0, r0 + SUBLANES)
            b_t, q_t = b3[:, rows, :], q3[:, rows, :]
            t_pos = r0 + t_idx[:, :SUBLANES, :]
            o_t = jnp.zeros((nb, SUBLANES, B_VAL_DIM), F32)
            for s in range(min(GLR_BLOCK, r0 + SUBLANES)):
                gap = b_t - b3[:, s:s + 1, :]
                dec = jnp.exp(gap if s <= r0 else jnp.where(t_pos >= s, gap, NEG_INF))
                a = jnp.sum(q_t * k3[:, s:s + 1, :] * dec, axis=-1, keepdims=True)
                o_t = o_t + a * v3[:, s:s + 1, :]
            o_tiles.append(o_t)
        o_in = jnp.concatenate(o_tiles, axis=1)

        v_t = bi.T
        lhs = jnp.where(in_block, v_t[None], 0.0).astype(BF16).reshape(nb * B_VAL_DIM, tt)
        u_t = _dot(lhs, k_dec).reshape(nb, B_VAL_DIM, B_KEY_DIM)
        decay_end = jnp.exp(b_end)
        st = st_ref[hh]
        outs = []
        for jb in range(nb):
            outs.append(_dot_nt(q_dec[jb], st.astype(BF16)))
            st = st * decay_end[jb] + u_t[jb]
        st_ref[hh] = st
        o = jnp.concatenate(outs, axis=0) + o_in.reshape(tt, B_VAL_DIM)
        o = _rms(o, gw) * (bo * jax.nn.sigmoid(bo))
        o_ref[:, B_VAL_DIM * hh:B_VAL_DIM * (hh + 1)] = o.astype(o_ref.dtype)

    @pl.when(it == pl.num_programs(1) - 1)
    def _():
        for hh in range(B_HEADS):
            s_ref[hh] = st_ref[hh].T.astype(s_ref.dtype)


def _glr(bg4, lb_logits, glr_norm_w, s0, t_valid, tt, layer):
    b, t, _ = bg4.shape
    state = pl.BlockSpec((None, B_HEADS, B_KEY_DIM, B_VAL_DIM), lambda bi, i: (bi, 0, 0, 0))
    return pl.pallas_call(
        functools.partial(_glr_kernel, t_valid, layer),
        grid=(b, t // tt),
        in_specs=[pl.BlockSpec((None, tt, IN_GROUP_B), lambda bi, i: (bi, i, 0)),
                  pl.BlockSpec(lb_logits.shape, lambda bi, i: (0, 0)),
                  pl.BlockSpec((1, B_VAL_DIM), lambda bi, i: (0, 0)),
                  state],
        out_specs=(pl.BlockSpec((None, tt, B_V), lambda bi, i: (bi, i, 0)), state),
        out_shape=(jax.ShapeDtypeStruct((b, t, B_V), BF16), jax.ShapeDtypeStruct(s0.shape, s0.dtype)),
        scratch_shapes=[pltpu.VMEM((B_HEADS, B_VAL_DIM, B_KEY_DIM), F32)],
        compiler_params=_params("parallel", "arbitrary"),
    )(bg4, lb_logits, glr_norm_w, s0)


def _pool_kernel(n_hist, x_ref, prev_ref, g_ref, wp_ref, sc_ref, o_ref, hist_ref, carry_ref):
    tt, d = x_ref.shape
    group = d // len(POOL_WINDOWS)
    it = pl.program_id(1)

    @pl.when(it == 0)
    def _():
        carry_ref[...] = prev_ref[...]

    x = x_ref[...]
    h = _rms(x, g_ref[...])
    ext = jnp.concatenate([carry_ref[...], h], axis=0)
    pos = it * tt + lax.broadcasted_iota(jnp.int32, (tt, 1), 0) + n_hist
    run = ext
    width = 1
    for gi, win in enumerate(POOL_WINDOWS):
        while width < win:
            run = run + pltpu.roll(run, width, 0)
            width *= 2
        sl = slice(group * gi, group * (gi + 1))
        cnt = jnp.minimum(pos + 1, win).astype(F32)
        z = run[POOL_HALO:, sl] / cnt - h[:, sl]
        y = _dot(z.astype(BF16), wp_ref[gi]) * sc_ref[:, sl]
        o_ref[:, sl] = x[:, sl] + y
    tail = ext[tt:, :]
    carry_ref[...] = tail
    hist_ref[...] = tail


def _pool(x, prev, g, w_pool, scale, n_hist, tt):
    b, t, d = x.shape
    hist_spec = pl.BlockSpec((None, POOL_HALO, d), lambda bi, i: (bi, 0, 0))
    return pl.pallas_call(
        functools.partial(_pool_kernel, n_hist),
        grid=(b, t // tt),
        in_specs=[pl.BlockSpec((None, tt, d), lambda bi, i: (bi, i, 0)), hist_spec,
                  pl.BlockSpec((1, d), lambda bi, i: (0, 0)),
                  pl.BlockSpec(w_pool.shape, lambda bi, i: (0, 0, 0)),
                  pl.BlockSpec((1, d), lambda bi, i: (0, 0))],
        out_specs=(pl.BlockSpec((None, tt, d), lambda bi, i: (bi, i, 0)), hist_spec),
        out_shape=(jax.ShapeDtypeStruct((b, t, d), F32), jax.ShapeDtypeStruct((b, POOL_HALO, d), F32)),
        scratch_shapes=[pltpu.VMEM((POOL_HALO, d), F32)],
        compiler_params=_params("parallel", "arbitrary"),
    )(x, prev, g, w_pool, scale)


def _ffn_ple_kernel(n_chunks, has_mix, final, *refs):
    if has_mix:
        att_ref, glr_ref, wa_ref, wb_ref = refs[:4]
        refs = refs[4:]
    x_ref, p_ref, gf_ref, wgu_ref, wd_ref, gp_ref, wpg_ref, wpe_ref, gfin_ref, o_ref = refs
    x = x_ref[...]
    if has_mix:
        x = x + _dot(att_ref[...], wa_ref[...]) + _dot(glr_ref[...], wb_ref[...])
    h = _rms(x, gf_ref[...]).astype(BF16)
    d_ff = wd_ref.shape[0]
    cw = d_ff // n_chunks
    acc = None
    for c in range(n_chunks):
        sl = slice(cw * c, cw * (c + 1))
        gate = _dot(h, wgu_ref[:, sl])
        up = _dot(h, wgu_ref[:, d_ff + cw * c:d_ff + cw * (c + 1)])
        part = _dot((gate * jax.nn.sigmoid(gate) * up).astype(BF16), wd_ref[sl, :])
        acc = part if acc is None else acc + part
    x = x + acc
    gate = jax.nn.sigmoid(_dot(_rms(x, gp_ref[...]).astype(BF16), wpg_ref[...]))
    x = x + _dot(p_ref[...].astype(BF16), wpe_ref[...]) * gate
    if final:
        x = _rms(x, gfin_ref[...])
    o_ref[...] = x


def _ffn_ple(x, mix, p, g_ffn, w_gate_up, w_down, g_ple, w_ple_gate, w_ple, g_final, final, tm):
    n, d = x.shape
    d_ff = w_down.shape[0]
    n_chunks = 2 if d_ff % (2 * LANES) == 0 else 1
    row = lambda width: pl.BlockSpec((tm, width), lambda i: (i, 0))
    args = [x, p, g_ffn, w_gate_up, w_down, g_ple, w_ple_gate, w_ple, g_final]
    specs = [row(d), row(p.shape[1])] + [_const_spec(a.shape) for a in args[2:]]
    if mix is not None:
        args = list(mix) + args
        specs = [row(mix[0].shape[1]), row(mix[1].shape[1]), _const_spec(mix[2].shape), _const_spec(mix[3].shape)] + specs
    return pl.pallas_call(
        functools.partial(_ffn_ple_kernel, n_chunks, mix is not None, final),
        grid=(n // tm,),
        in_specs=specs,
        out_specs=row(d),
        out_shape=jax.ShapeDtypeStruct((n, d), F32),
        compiler_params=_params("parallel"),
    )(*args)


def _rope_tables(pos):
    half = A_HEAD_DIM // 2
    inv = ROPE_THETA ** (-jnp.arange(half, dtype=F32) / half)
    ang = pos.astype(F32)[:, None] * inv[None, :]
    c, s = jnp.cos(ang), jnp.sin(ang)
    reps = LANES // A_HEAD_DIM
    return jnp.tile(jnp.concatenate([c, c], axis=1), (1, reps)), jnp.tile(jnp.concatenate([-s, s], axis=1), (1, reps))


def _pack_w_in(w):
    d = w.shape[0]
    n_a = 3 * A_Q + IDX_Q
    n_i = IDX_DIM + IDX_HEADS
    return jnp.concatenate([w[:, :n_a], w[:, n_a:n_a + n_i], jnp.zeros((d, IN_GROUP_I - n_i), w.dtype),
                            w[:, n_a + n_i:]], axis=1).astype(BF16)


FFN_ROWS = 512


def _row_tile(n, want):
    t = min(n, want)
    assert n % t == 0
    return t


def _trunk(x, p, pos, attend, glr_s0, glr_tile, pool_prev, pool_hist, W):
    b, t, d = x.shape
    n = b * t
    tm = _row_tile(n, 256)
    row2 = lambda a: a.reshape(1, -1)
    cos_tab, sin_tab = _rope_tables(pos)
    depth = W["mix_norm"].shape[0]
    xf = x.reshape(n, d)
    outs = {}
    for i in range(depth):
        jj = i // 2
        if i % 2 == 0:
            bs, ts = (b, t) if t % tm == 0 else (1, n)
            res = _in_proj(xf.reshape(bs, ts, d), row2(W["mix_norm"][i]), _pack_w_in(W["w_in"][jj]),
                           cos_tab, sin_tab, tm)
            flat = lambda a: a.reshape(n, a.shape[2])
            qb, kt, kb, vt, vb, vtb, qib, kiw, kit, wt, ki2, bg4 = res
            qb, kb, vb, qib, kiw, ki2, bg4 = [flat(a) for a in (qb, kb, vb, qib, kiw, ki2, bg4)]
            att = attend(jj, qb, qib, kiw, ki2, kb, vb, wt, vtb)
            t_pad = -(-t // glr_tile) * glr_tile
            bg3 = bg4.reshape(b, t, IN_GROUP_B)
            if t_pad != t:
                bg3 = jnp.pad(bg3, ((0, 0), (0, t_pad - t), (0, 0)))
            glr, s_new = _glr(bg3, W["lb_logits"], row2(W["glr_norm"][jj]), glr_s0[jj], t, glr_tile, jj)
            glr = glr[:, :t].reshape(n, B_V)
            w_out = W["w_out"][jj].astype(BF16)
            mix = (att, glr, w_out[:A_Q], w_out[A_Q:])
            heads = lambda a: a.reshape(bs, A_HEADS, A_HEAD_DIM, ts).transpose(0, 3, 1, 2).reshape(
                b, t, A_HEADS, A_HEAD_DIM)
            outs.setdefault("k", []).append(heads(kt))
            outs.setdefault("v", []).append(heads(vt))
            outs.setdefault("ki", []).append(kit.transpose(0, 2, 1).reshape(b, t, IDX_DIM))
            outs.setdefault("s", []).append(s_new)
        else:
            tt = _row_tile(t, 256)
            t_pad = -(-t // 8) * 8
            x3 = xf.reshape(b, t, d)
            if t_pad != t:
                x3 = jnp.pad(x3, ((0, 0), (0, t_pad - t), (0, 0)))
                tt = t_pad
            y3, hist = _pool(x3, pool_prev[jj], row2(W["mix_norm"][i]), W["pool_w"][jj].astype(BF16),
                             row2(W["pool_scale"][jj]), pool_hist, tt)
            xf = y3[:, :t].reshape(n, d)
            mix = None
            outs.setdefault("hist", []).append((hist, t_pad))
        xf = _ffn_ple(xf, mix, p[i].reshape(n, -1), row2(W["ffn_norm"][i]), W["w_up"][i].astype(BF16),
                      W["w_down"][i].astype(BF16), row2(W["ple_norm"][i]), W["w_ple_gate"][i].astype(BF16),
                      W["w_ple"][i].astype(BF16), row2(W["final_norm"]), i == depth - 1, _row_tile(n, FFN_ROWS))
    return xf.reshape(b, t, d), outs


def kernel(x_prompt, x_sample, cache_k, cache_v, cache_kidx, state_glr, state_pool, page_table, p_prompt, p_sample,
           mix_norm, w_in, w_out, lb_logits, glr_norm, pool_w, pool_scale, ffn_norm, w_up, w_down, ple_norm,
           w_ple_gate, w_ple, final_norm):
    W = dict(mix_norm=mix_norm, w_in=w_in, w_out=w_out, lb_logits=lb_logits, glr_norm=glr_norm, pool_w=pool_w,
             pool_scale=pool_scale, ffn_norm=ffn_norm, w_up=w_up, w_down=w_down, ple_norm=ple_norm,
             w_ple_gate=w_ple_gate, w_ple=w_ple, final_norm=final_norm)
    n_ab = w_in.shape[0]
    n_c = pool_w.shape[0]
    d = x_prompt.shape[-1]
    n_buf = max(POOL_WINDOWS) - 1

    bp, sp = x_prompt.shape[:2]

    def attend_prompt(jj, qb, qib, kiw, ki2, kb, vb, wt, vtb):
        r3 = lambda a: a.reshape(bp, sp, a.shape[-1])
        return _dsa_prompt(r3(qb), r3(qib), wt, r3(ki2), r3(kb), vtb, vtb.shape[-1]).reshape(bp * sp, A_Q)

    glr0_p = [jnp.zeros((bp, B_HEADS, B_KEY_DIM, B_VAL_DIM), state_glr.dtype) for _ in range(n_ab)]
    pool0_p = [jnp.zeros((bp, POOL_HALO, d), F32) for _ in range(n_c)]
    yp, op = _trunk(x_prompt, p_prompt, jnp.arange(sp, dtype=jnp.int32), attend_prompt, glr0_p,
                    _row_tile(sp, 256), pool0_p, 0, W)

    bd, ts = x_sample.shape[:2]
    past = page_table.shape[1] * PAGE_SIZE

    n_pool = cache_k.shape[1]
    cache_kt = jnp.transpose(cache_k, (0, 1, 3, 4, 2)).reshape(n_ab, n_pool, A_Q, PAGE_SIZE)
    cache_vt = jnp.transpose(cache_v, (0, 1, 3, 4, 2)).reshape(n_ab, n_pool, A_Q, PAGE_SIZE)
    cache_kit = jnp.transpose(cache_kidx, (0, 1, 3, 2))

    def attend_sample(jj, qb, qib, kiw, ki2, kb, vb, wt, vtb):
        return _dsa_sample(qb, qib, kiw, ki2, kb, vb, cache_kt[jj], cache_vt[jj], cache_kit[jj], page_table, ts)

    pos_s = jnp.tile(past + jnp.arange(ts, dtype=jnp.int32), bd)
    pool0_s = [jnp.pad(state_pool[j].astype(F32), ((0, 0), (POOL_HALO - n_buf, 0), (0, 0))) for j in range(n_c)]
    ys, os_ = _trunk(x_sample, p_sample, pos_s, attend_sample, [state_glr[j] for j in range(n_ab)],
                     LANES, pool0_s, n_buf, W)

    def pool_rows(o, prev, t):
        res = []
        for j, (hist, t_pad) in enumerate(o["hist"]):
            if t >= n_buf:
                res.append(hist[:, POOL_HALO - (t_pad - t) - n_buf:POOL_HALO - (t_pad - t)])
            else:
                new = hist[:, POOL_HALO - t_pad:POOL_HALO - t_pad + t]
                res.append(jnp.concatenate([prev[j][:, t:].astype(new.dtype), new], axis=1))
        return jnp.stack(res)

    return (yp, ys, jnp.stack(op["k"]), jnp.stack(op["v"]), jnp.stack(op["ki"]), jnp.stack(op["s"]),
            pool_rows(op, None, sp).astype(x_prompt.dtype),
            jnp.stack(os_["k"]), jnp.stack(os_["v"]), jnp.stack(os_["ki"]), jnp.stack(os_["s"]),
            pool_rows(os_, state_pool, ts).astype(x_sample.dtype))
```

```python
import functools

import jax
import jax.numpy as jnp
from jax import lax
from jax.experimental import pallas as pl
from jax.experimental.pallas import tpu as pltpu

F32 = jnp.float32
BF16 = jnp.bfloat16

A_HEADS = 8
A_HEAD_DIM = 64
IDX_HEADS = 8
IDX_DIM = 64
TOPK_MAX = 256
ROPE_THETA = 10000.0
B_HEADS = 4
B_KEY_DIM = 128
B_VAL_DIM = 128
GLR_BLOCK = 16
POOL_WINDOWS = (2, 4, 8, 16)
POOL_HALO = 16
PAGE_SIZE = 128
RMS_EPS = 1e-6

A_Q = A_HEADS * A_HEAD_DIM
IDX_Q = IDX_HEADS * IDX_DIM
B_K = B_HEADS * B_KEY_DIM
B_V = B_HEADS * B_VAL_DIM

LANES = 128
VMEM_LIMIT = 56 * 1024 * 1024
INT_MIN = -(2 ** 31)
NEG_INF = float("-inf")


def _params(*sem):
    return pltpu.CompilerParams(dimension_semantics=sem, vmem_limit_bytes=VMEM_LIMIT)


def _rms(x, g):
    ms = jnp.mean(x * x, axis=-1, keepdims=True)
    return x * lax.rsqrt(ms + RMS_EPS) * g


def _dot(a, b):
    return jnp.dot(a, b, preferred_element_type=F32)


def _dot_nt(a, b):
    return lax.dot_general(a, b, (((1,), (1,)), ((), ())), preferred_element_type=F32)


def _const_spec(shape):
    nd = len(shape)
    return pl.BlockSpec(shape, lambda *_: (0,) * nd, pipeline_mode=pl.Buffered(1))


IN_GROUP_A = 4 * A_Q
IN_GROUP_I = LANES
IN_GROUP_B = 2 * B_K + 2 * B_V
IN_PACKED = IN_GROUP_A + IN_GROUP_I + IN_GROUP_B


def _in_proj_kernel(x_ref, g_ref, w_ref, cos_ref, sin_ref,
                    qb_ref, kt_ref, kb_ref, vt_ref, vb_ref, vtb_ref, qib_ref, kiw_ref, kit_ref, wt_ref, ki2_ref,
                    bg_ref):
    tm = x_ref.shape[0]
    h = _rms(x_ref[...], g_ref[...]).astype(BF16)
    cos = cos_ref[...]
    sin = sin_ref[...]
    lane = lax.broadcasted_iota(jnp.int32, (tm, LANES), 1)
    first = (lane % A_HEAD_DIM) < (A_HEAD_DIM // 2)
    slab = lambda i: slice(LANES * i, LANES * (i + 1))

    def rope(z, c, s):
        rot = jnp.where(first, pltpu.roll(z, LANES - A_HEAD_DIM // 2, 1), pltpu.roll(z, A_HEAD_DIM // 2, 1))
        return z * c + rot * s

    def roped(col0):
        z = _dot(h, w_ref[:, col0:col0 + A_Q])
        return [rope(z[:, slab(i)], cos, sin) for i in range(A_Q // LANES)]

    scale = A_HEAD_DIM ** -0.5
    for i, r in enumerate(roped(0)):
        qb_ref[:, slab(i)] = (r * scale).astype(BF16)
    for i, r in enumerate(roped(A_Q)):
        kt_ref[slab(i), :] = r.T
        kb_ref[:, slab(i)] = r.astype(BF16)
    v = _dot(h, w_ref[:, 2 * A_Q:3 * A_Q])
    for i in range(A_Q // LANES):
        v_t = v[:, slab(i)].T
        vt_ref[slab(i), :] = v_t
        vtb_ref[slab(i), :] = v_t.astype(BF16)
    vb_ref[...] = v.astype(BF16)
    for i, r in enumerate(roped(3 * A_Q)):
        qib_ref[:, slab(i)] = (r * (IDX_DIM ** -0.5)).astype(BF16)
    is_key = lane < IDX_DIM
    z = _dot(h, w_ref[:, IN_GROUP_A:IN_GROUP_A + IN_GROUP_I])
    kiw = rope(z, jnp.where(is_key, cos, 1.0), jnp.where(is_key, sin, 0.0))
    kiw_ref[...] = kiw
    kiw_t = kiw.T
    kit_ref[...] = kiw_t[:IDX_DIM, :]
    wt_ref[...] = kiw_t[IDX_DIM:IDX_DIM + IDX_HEADS, :]
    ki2_ref[...] = jnp.where(is_key, kiw, pltpu.roll(kiw, IDX_DIM, 1)).astype(BF16)
    col0 = IN_GROUP_A + IN_GROUP_I
    for i in range(IN_GROUP_B // A_Q):
        bg_ref[:, A_Q * i:A_Q * (i + 1)] = _dot(h, w_ref[:, col0 + A_Q * i:col0 + A_Q * (i + 1)])


def _in_proj(x, g, w_packed, cos_tab, sin_tab, tm):
    b, t, d = x.shape
    row = lambda width: pl.BlockSpec((None, tm, width), lambda bi, i: (bi, i, 0))
    col = lambda height: pl.BlockSpec((None, height, tm), lambda bi, i: (bi, 0, i))
    tab = pl.BlockSpec((tm, LANES), lambda bi, i: (i, 0))
    rows = lambda width, dt: (jax.ShapeDtypeStruct((b, t, width), dt), row(width))
    cols = lambda height: (jax.ShapeDtypeStruct((b, height, t), F32), col(height))
    outs = (
        rows(A_Q, BF16),
        cols(A_Q),
        rows(A_Q, BF16),
        cols(A_Q),
        rows(A_Q, BF16),
        (jax.ShapeDtypeStruct((b, t // tm, A_Q, tm), BF16),
         pl.BlockSpec((None, None, A_Q, tm), lambda bi, i: (bi, i, 0, 0))),
        rows(IDX_Q, BF16),
        rows(LANES, F32),
        cols(IDX_DIM),
        cols(IDX_HEADS),
        rows(LANES, BF16),
        rows(IN_GROUP_B, F32),
    )
    return pl.pallas_call(
        _in_proj_kernel,
        grid=(b, t // tm),
        in_specs=[row(d), _const_spec((1, d)), _const_spec((d, IN_PACKED)), tab, tab],
        out_specs=tuple(o[1] for o in outs),
        out_shape=tuple(o[0] for o in outs),
        compiler_params=_params("parallel", "parallel"),
    )(x, g, w_packed, cos_tab, sin_tab)


def _level_of(code):
    signed = code ^ jnp.int32(INT_MIN)
    level = pltpu.bitcast(signed ^ ((signed >> 31) & jnp.int32(0x7FFFFFFF)), F32)
    return jnp.where(level != level, jnp.where(signed < 0, NEG_INF, -NEG_INF), level)


def _unrolled(n, body, init):
    for c in range(n):
        init = body(c, init)
    return init


def _fold_lanes(t):
    acc = t[:, :LANES]
    for i in range(1, t.shape[1] // LANES):
        acc = acc + t[:, LANES * i:LANES * (i + 1)]
    return acc


def _select_bias(n_c, width, get_key, get_bias, set_bias, kk, use_topk):
    rows = use_topk.shape[0]
    kf = float(kk)
    zeros = jnp.zeros((rows, LANES), F32)

    def count(cmp, level):
        level_b = jnp.broadcast_to(level, (rows, LANES))
        parts = [zeros] * 4
        for c in range(n_c):
            k = get_key(c)
            for i in range(width // LANES):
                hit = jnp.where(cmp(k[:, LANES * i:LANES * (i + 1)], level_b), 1.0, 0.0)
                parts[(c + i) % 4] = parts[(c + i) % 4] + hit
        return jnp.sum((parts[0] + parts[1]) + (parts[2] + parts[3]), axis=1, keepdims=True)

    def bisect(i, t):
        cand = t | jnp.left_shift(jnp.int32(1), 31 - i)
        return jnp.where(count(jnp.greater_equal, _level_of(cand)) >= kf, cand, t)

    thr = _level_of(lax.fori_loop(0, 32, bisect, jnp.zeros((rows, 1), jnp.int32)))

    def apply(c, acc):
        ge = get_key(c) >= thr
        set_bias(c, jnp.where(use_topk, jnp.where(ge, 0.0, NEG_INF), get_bias(c)))
        return acc + _fold_lanes(jnp.where(ge, 1.0, 0.0))

    cnt_ge = jnp.sum(_unrolled(n_c, apply, zeros), axis=1, keepdims=True)
    tied = jnp.logical_and(use_topk, cnt_ge != kf)

    @pl.when(jnp.max(jnp.where(tied, 1.0, 0.0)) > 0.0)
    def _():
        room = kf - count(jnp.greater, thr)
        before = (lax.broadcasted_iota(jnp.int32, (width, width), 0)
                  < lax.broadcasted_iota(jnp.int32, (width, width), 1))
        tri = jnp.where(before, 1.0, 0.0).astype(BF16)

        def fix(c, seen):
            kc = get_key(c)
            eq = jnp.where(kc == thr, 1.0, 0.0)
            rank = _dot(eq.astype(BF16), tri) + seen
            keep = jnp.logical_or(kc > thr, jnp.logical_and(kc == thr, rank < room))
            set_bias(c, jnp.where(use_topk, jnp.where(keep, 0.0, NEG_INF), get_bias(c)))
            return seen + jnp.sum(eq, axis=1, keepdims=True)

        _unrolled(n_c, fix, jnp.zeros((rows, 1), F32))


def _pair_masks(rows):
    lane = lax.broadcasted_iota(jnp.int32, (rows, LANES), 1)
    lo = lane < A_HEAD_DIM
    return lo, jnp.logical_not(lo)


M_INIT = -1e30


SUBLANES = 8
LOG2_E = 1.4426950408889634


def _fold_rows(a, op):
    return op(a.reshape(a.shape[0] // SUBLANES, SUBLANES, a.shape[1]), axis=0)


def _select_bias_t(n_c, key_ref, bias_ref, kk, use_topk):
    kc, rows = key_ref.shape[1:]
    kf = float(kk)
    zeros = jnp.zeros((SUBLANES, rows), F32)

    def count(cmp, level):
        level8 = jnp.broadcast_to(level, (SUBLANES, rows))[None, None]

        def body(c, a):
            k4 = key_ref[c].reshape(4, kc // (4 * SUBLANES), SUBLANES, rows)
            part = jnp.sum(jnp.where(cmp(k4, level8), 1.0, 0.0), axis=1)
            return a + ((part[0] + part[1]) + (part[2] + part[3]))

        return jnp.sum(lax.fori_loop(0, n_c, body, zeros), axis=0, keepdims=True)

    def bisect(i, t):
        cand = t | jnp.left_shift(jnp.int32(1), 31 - i)
        return jnp.where(count(jnp.greater_equal, _level_of(cand)) >= kf, cand, t)

    thr = _level_of(lax.fori_loop(0, 32, bisect, jnp.zeros((1, rows), jnp.int32)))

    def apply(c, acc):
        ge = key_ref[c] >= thr
        bias_ref[c] = jnp.where(use_topk, jnp.where(ge, 0.0, NEG_INF), bias_ref[c])
        return acc + _fold_rows(jnp.where(ge, 1.0, 0.0), jnp.sum)

    cnt_ge = jnp.sum(lax.fori_loop(0, n_c, apply, zeros), axis=0, keepdims=True)
    tied = jnp.logical_and(use_topk, cnt_ge != kf)

    @pl.when(jnp.max(jnp.where(tied, 1.0, 0.0)) > 0.0)
    def _():
        room = kf - count(jnp.greater, thr)
        earlier = (lax.broadcasted_iota(jnp.int32, (kc, kc), 1) < lax.broadcasted_iota(jnp.int32, (kc, kc), 0))
        tri = jnp.where(earlier, 1.0, 0.0).astype(BF16)

        def fix(c, seen):
            k = key_ref[c]
            eq = jnp.where(k == thr, 1.0, 0.0)
            rank = _dot(tri, eq.astype(BF16)) + seen
            keep = jnp.logical_or(k > thr, jnp.logical_and(k == thr, rank < room))
            bias_ref[c] = jnp.where(use_topk, jnp.where(keep, 0.0, NEG_INF), bias_ref[c])
            return seen + jnp.sum(eq, axis=0, keepdims=True)

        lax.fori_loop(0, n_c, fix, jnp.zeros((1, rows), F32))


def _dsa_prompt_kernel(kk, q_ref, qi_ref, wt_ref, ki2_ref, k_ref, vt_ref, o_ref,
                       key_ref, bias_ref, qh_ref, s_ref, max_ref, sum_ref, acc_ref):
    tq = q_ref.shape[0]
    kc = key_ref.shape[1]
    j = pl.program_id(1)
    n_c = j + 1
    qpos = j * tq + lax.broadcasted_iota(jnp.int32, (1, tq), 1)
    krow = lax.broadcasted_iota(jnp.int32, (kc, 1), 0)
    masks = _pair_masks(tq)
    slab = lambda p: slice(LANES * p, LANES * (p + 1))
    head_rows = lambda hh: slice(A_HEAD_DIM * hh, A_HEAD_DIM * (hh + 1))
    keys_of = lambda c: pl.ds(pl.multiple_of(c * kc, kc), kc)
    one_head = lambda ref, hh: jnp.where(masks[hh % 2], ref[:, slab(hh // 2)], jnp.zeros((), ref.dtype))
    causal = lambda c: c * kc + krow <= qpos

    def init_bias(c, _):
        bias_ref[c] = jnp.where(causal(c), 0.0, NEG_INF)
        return 0

    lax.fori_loop(0, n_c, init_bias, 0)

    @pl.when(n_c * kc > kk)
    def _():
        w = wt_ref[...] * (IDX_HEADS ** -0.5)
        for hh in range(IDX_HEADS):
            qh_ref[hh] = one_head(qi_ref, hh)

        def score_chunk(c, _):
            ki2 = ki2_ref[keys_of(c), :]
            acc = None
            for hh in range(IDX_HEADS):
                term = jnp.maximum(_dot_nt(ki2, qh_ref[hh]), 0.0) * w[hh:hh + 1, :]
                acc = term if acc is None else acc + term
            key_ref[c] = jnp.where(causal(c), acc, NEG_INF)
            return 0

        lax.fori_loop(0, n_c, score_chunk, 0)
        _select_bias_t(n_c, key_ref, bias_ref, kk, qpos >= kk)

    for hh in range(A_HEADS):
        qh_ref[hh] = one_head(q_ref, hh)
        max_ref[hh] = jnp.full((SUBLANES, tq), M_INIT, F32)
        sum_ref[hh] = jnp.zeros((SUBLANES, tq), F32)
    acc_ref[...] = jnp.zeros(acc_ref.shape, F32)

    def max_chunk(c, _):
        bias = bias_ref[c]
        for hh in range(A_HEADS):
            s = _dot_nt(k_ref[keys_of(c), slab(hh // 2)], qh_ref[hh]) * LOG2_E + bias
            s_ref[hh, c] = s
            max_ref[hh] = jnp.maximum(max_ref[hh], _fold_rows(s, jnp.max))
        return 0

    lax.fori_loop(0, n_c, max_chunk, 0)
    for hh in range(A_HEADS):
        max_ref[hh] = jnp.broadcast_to(jnp.max(max_ref[hh], axis=0, keepdims=True), (SUBLANES, tq))

    def sum_chunk(c, _):
        for hh in range(A_HEADS):
            e = jnp.exp2(s_ref[hh, c].reshape(kc // SUBLANES, SUBLANES, tq) - max_ref[hh][None])
            sum_ref[hh] = sum_ref[hh] + jnp.sum(e, axis=0)
            pv = _dot(vt_ref[c, head_rows(hh), :], e.reshape(kc, tq).astype(BF16))
            acc_ref[head_rows(hh), :] = acc_ref[head_rows(hh), :] + pv
        return 0

    lax.fori_loop(0, n_c, sum_chunk, 0)
    for p in range(A_Q // LANES):
        parts = [acc_ref[head_rows(hh), :] / jnp.sum(sum_ref[hh], axis=0, keepdims=True) for hh in (2 * p, 2 * p + 1)]
        o_ref[:, slab(p)] = jnp.concatenate(parts, axis=0).T.astype(o_ref.dtype)


def _dsa_prompt(qb, qib, wt, ki2, kb, vtb, tq):
    b, s, _ = qb.shape
    n_c = s // tq
    assert vtb.shape == (b, n_c, A_Q, tq)
    kk = min(TOPK_MAX, s // 4)
    tile = lambda width: pl.BlockSpec((None, tq, width), lambda bi, j: (bi, j, 0))
    full = lambda width: pl.BlockSpec((None, s, width), lambda bi, j: (bi, 0, 0))
    return pl.pallas_call(
        functools.partial(_dsa_prompt_kernel, kk),
        grid=(b, n_c),
        in_specs=[tile(A_Q), tile(IDX_Q), pl.BlockSpec((None, IDX_HEADS, tq), lambda bi, j: (bi, 0, j)),
                  full(LANES), full(A_Q), pl.BlockSpec((None, n_c, A_Q, tq), lambda bi, j: (bi, 0, 0, 0))],
        out_specs=tile(A_Q),
        out_shape=jax.ShapeDtypeStruct((b, s, A_Q), BF16),
        scratch_shapes=[pltpu.VMEM((n_c, tq, tq), F32), pltpu.VMEM((n_c, tq, tq), F32),
                        pltpu.VMEM((A_HEADS, tq, LANES), BF16), pltpu.VMEM((A_HEADS, n_c, tq, tq), F32),
                        pltpu.VMEM((A_HEADS, SUBLANES, tq), F32), pltpu.VMEM((A_HEADS, SUBLANES, tq), F32),
                        pltpu.VMEM((A_Q, tq), F32)],
        compiler_params=_params("parallel", "arbitrary"),
    )(qb, qib, wt, ki2, kb, vtb)


TPAD = 8
NEW_PAD = LANES
CHUNK_PAGES = 16
PAGE_UNROLL = 8
RING_SLOTS = 6


def _dsa_sample_kernel(t_new, n_pages, kk, pt_ref, qbd_ref, qir_ref, wr_ref, knew_ref, vnew_ref, kinew_ref,
                       ckidx_ref, ck_ref, cv_ref, o_ref,
                       kibuf, kvbuf, logit_ref, key_ref, bias_ref, sem_ki, sem_kv):
    b = pl.program_id(0)
    past = n_pages * PAGE_SIZE
    rows = A_HEADS * TPAD
    n_chunks = n_pages // CHUNK_PAGES

    def ki_copy(p):
        return pltpu.make_async_copy(ckidx_ref.at[pt_ref[b, p]], kibuf.at[p], sem_ki)

    def kv_copy(src_ref, c, i, slot):
        return pltpu.make_async_copy(src_ref.at[pt_ref[b, c * CHUNK_PAGES + i]], kvbuf.at[slot, i], sem_kv.at[slot])

    def start_chunk(src_ref, c, slot):
        lax.fori_loop(0, CHUNK_PAGES, lambda i, _: (kv_copy(src_ref, c, i, slot).start(), 0)[1], 0)

    def wait_chunk(src_ref, c, slot):
        lax.fori_loop(0, CHUNK_PAGES, lambda i, _: (kv_copy(src_ref, c, i, slot).wait(), 0)[1], 0)

    n_xfer = 2 * n_chunks

    def xfer(t):
        return (ck_ref, t, t % RING_SLOTS) if t < n_chunks else (cv_ref, t - n_chunks, t % RING_SLOTS)

    lax.fori_loop(0, n_pages, lambda p, _: (ki_copy(p).start(), 0)[1], 0)
    for t in range(min(RING_SLOTS, n_xfer)):
        start_chunk(*xfer(t))
    lax.fori_loop(0, n_pages, lambda p, _: (ki_copy(p).wait(), 0)[1], 0)

    qir = qir_ref[...]
    w = wr_ref[...] * (IDX_HEADS ** -0.5)
    tok = lax.broadcasted_iota(jnp.int32, (TPAD, 1), 0)
    tok_ok = tok < t_new

    def keys_of(s, ok):
        t = jnp.maximum(s, 0.0) * w
        return jnp.where(ok, jnp.sum(t.reshape(IDX_HEADS, TPAD, LANES), axis=0), NEG_INF)

    def index_page(p, _):
        key_ref[p] = keys_of(_dot(qir, kibuf[p].astype(BF16)), tok_ok)
        return 0

    lax.fori_loop(0, n_pages, index_page, 0, unroll=PAGE_UNROLL)
    new_idx = lax.broadcasted_iota(jnp.int32, (1, NEW_PAD), 1)
    new_ok = jnp.logical_and(jnp.logical_and(new_idx <= tok, new_idx < t_new), tok_ok)
    key_ref[n_pages] = keys_of(_dot_nt(qir, kinew_ref[...]), new_ok)
    bias_ref[0:n_pages] = jnp.broadcast_to(jnp.where(tok_ok, 0.0, NEG_INF)[None], (n_pages, TPAD, LANES))
    bias_ref[n_pages] = jnp.where(new_ok, 0.0, NEG_INF)

    def set_bias(c, val):
        bias_ref[c] = val

    use_topk = jnp.logical_and(past + tok + 1 > kk, tok_ok)
    _select_bias(n_pages + 1, LANES, lambda c: key_ref[c], lambda c: bias_ref[c], set_bias, kk, use_topk)

    qbd = qbd_ref[...]
    for c in range(n_chunks):
        slot = c % RING_SLOTS
        wait_chunk(*xfer(c))

        def logits_page(i, _, c=c, slot=slot):
            logit_ref[c * CHUNK_PAGES + i] = _dot(qbd, kvbuf[slot, i].astype(BF16))
            return 0

        lax.fori_loop(0, CHUNK_PAGES, logits_page, 0, unroll=PAGE_UNROLL)
        if c + RING_SLOTS < n_xfer:
            start_chunk(*xfer(c + RING_SLOTS))
    logit_ref[n_pages] = _dot_nt(qbd, knew_ref[...])
    s = logit_ref[...].reshape(n_pages + 1, A_HEADS, TPAD, LANES) + bias_ref[...][:, None]
    s = jnp.where(tok_ok[None, None], s, 0.0).reshape(n_pages + 1, rows, LANES)
    m = jnp.max(jnp.max(s, axis=0), axis=1, keepdims=True)
    e = jnp.exp(s - m[None])
    l = jnp.sum(jnp.sum(e, axis=0), axis=1, keepdims=True)
    logit_ref[...] = e

    acc = _dot(logit_ref[n_pages].astype(BF16), vnew_ref[...])
    for c in range(n_chunks):
        slot = (n_chunks + c) % RING_SLOTS
        wait_chunk(*xfer(n_chunks + c))

        def value_page(i, a, c=c, slot=slot):
            return a + _dot_nt(logit_ref[c * CHUNK_PAGES + i].astype(BF16), kvbuf[slot, i].astype(BF16))

        acc = lax.fori_loop(0, CHUNK_PAGES, value_page, acc, unroll=PAGE_UNROLL)
        if n_chunks + c + RING_SLOTS < n_xfer:
            start_chunk(*xfer(n_chunks + c + RING_SLOTS))
    acc = (acc / l).reshape(A_HEADS, TPAD, A_Q)
    head_of_lane = lax.broadcasted_iota(jnp.int32, (TPAD, A_Q), 1) // A_HEAD_DIM
    out = jnp.zeros((TPAD, A_Q), F32)
    for hh in range(A_HEADS):
        out = out + jnp.where(head_of_lane == hh, acc[hh], 0.0)
    o_ref[...] = out.astype(o_ref.dtype)


def _dsa_sample(qb, qib, kiw, ki2, kb, vb, cache_kt, cache_vt, cache_kit, page_table, t_new):
    bd, n_pages = page_table.shape
    assert n_pages % CHUNK_PAGES == 0 and t_new <= TPAD
    past = n_pages * PAGE_SIZE
    kk = min(TOPK_MAX, (past + t_new) // 4)
    rows = A_HEADS * TPAD

    def pad_tokens(a, to):
        a = a.reshape(bd, t_new, a.shape[-1])
        return jnp.pad(a, ((0, 0), (0, to - t_new), (0, 0)))

    q8 = pad_tokens(qb, TPAD)
    head_of_lane = jnp.arange(A_Q) // A_HEAD_DIM
    qbd = jnp.where(head_of_lane[None, None, None, :] == jnp.arange(A_HEADS)[None, :, None, None],
                    q8[:, None], jnp.zeros((), BF16)).reshape(bd, rows, A_Q)
    qir = pad_tokens(qib, TPAD).reshape(bd, TPAD, IDX_HEADS, IDX_DIM).transpose(0, 2, 1, 3).reshape(bd, rows, IDX_DIM)
    wr = pad_tokens(kiw[:, IDX_DIM:IDX_DIM + IDX_HEADS], TPAD).transpose(0, 2, 1).reshape(bd, rows, 1)
    knew = pad_tokens(kb, NEW_PAD)
    vnew = pad_tokens(vb, NEW_PAD)
    kinew = pad_tokens(ki2[:, :IDX_DIM], NEW_PAD)

    per_seq = lambda r, c: pl.BlockSpec((None, r, c), lambda b, pt: (b, 0, 0))
    any_spec = pl.BlockSpec(memory_space=pl.ANY)
    grid_spec = pltpu.PrefetchScalarGridSpec(
        num_scalar_prefetch=1,
        grid=(bd,),
        in_specs=[per_seq(rows, A_Q), per_seq(rows, IDX_DIM), per_seq(rows, 1), per_seq(NEW_PAD, A_Q),
                  per_seq(NEW_PAD, A_Q), per_seq(NEW_PAD, IDX_DIM), any_spec, any_spec, any_spec],
        out_specs=per_seq(TPAD, A_Q),
        scratch_shapes=[
            pltpu.VMEM((n_pages, IDX_DIM, PAGE_SIZE), F32),
            pltpu.VMEM((RING_SLOTS, CHUNK_PAGES, A_Q, PAGE_SIZE), F32),
            pltpu.VMEM((n_pages + 1, rows, LANES), F32),
            pltpu.VMEM((n_pages + 1, TPAD, LANES), F32),
            pltpu.VMEM((n_pages + 1, TPAD, LANES), F32),
            pltpu.SemaphoreType.DMA(()),
            pltpu.SemaphoreType.DMA((RING_SLOTS,)),
        ],
    )
    out = pl.pallas_call(
        functools.partial(_dsa_sample_kernel, t_new, n_pages, kk),
        grid_spec=grid_spec,
        out_shape=jax.ShapeDtypeStruct((bd, TPAD, A_Q), BF16),
        compiler_params=_params("arbitrary"),
    )(page_table, qbd, qir, wr, knew, vnew, kinew, cache_kit, cache_kt, cache_vt)
    return out[:, :t_new].reshape(bd * t_new, A_Q)


def _glr_kernel(t_valid, layer, bg_ref, lbl_ref, gw_ref, s0_ref, o_ref, s_ref, st_ref):
    tt = bg_ref.shape[0]
    nb = tt // GLR_BLOCK
    it = pl.program_id(1)

    @pl.when(it == 0)
    def _():
        for hh in range(B_HEADS):
            st_ref[hh] = s0_ref[hh].astype(F32).T

    logits = lbl_ref[...]
    ex = jnp.exp(logits - jnp.max(logits, axis=0, keepdims=True))
    lb_all = jnp.sum(ex[:layer + 1], axis=0, keepdims=True) / jnp.sum(ex, axis=0, keepdims=True)

    row = lax.broadcasted_iota(jnp.int32, (tt, 1), 0)
    valid = (it * tt + row) < t_valid
    rloc = row % GLR_BLOCK
    t_idx = lax.broadcasted_iota(jnp.int32, (1, GLR_BLOCK, 1), 1)
    blk_of_col = lax.broadcasted_iota(jnp.int32, (nb, 1, tt), 2) // GLR_BLOCK
    blk_id = lax.broadcasted_iota(jnp.int32, (nb, 1, tt), 0)
    in_block = blk_of_col == blk_id
    gw = gw_ref[...]

    for hh in range(B_HEADS):
        lb = lb_all[:, B_KEY_DIM * hh:B_KEY_DIM * (hh + 1)]
        bq = bg_ref[:, B_KEY_DIM * hh:B_KEY_DIM * (hh + 1)]
        bf = bg_ref[:, B_K + B_KEY_DIM * hh:B_K + B_KEY_DIM * (hh + 1)]
        bi = bg_ref[:, 2 * B_K + B_VAL_DIM * hh:2 * B_K + B_VAL_DIM * (hh + 1)]
        bo = bg_ref[:, 2 * B_K + B_V + B_VAL_DIM * hh:2 * B_K + B_V + B_VAL_DIM * (hh + 1)]
        f = lb + (1.0 - lb) * jax.nn.sigmoid(bf)
        kx = jnp.where(valid, (1.0 - lb) * jax.nn.sigmoid(-bf), 0.0)
        qx = bq * jax.nn.sigmoid(bq)
        cum = jnp.where(valid, jnp.log(f), 0.0)
        for sh in (1, 2, 4, 8):
            cum = cum + jnp.where(rloc >= sh, pltpu.roll(cum, sh, 0), 0.0)
        b3 = cum.reshape(nb, GLR_BLOCK, B_KEY_DIM)
        q3 = qx.reshape(nb, GLR_BLOCK, B_KEY_DIM)
        k3 = kx.reshape(nb, GLR_BLOCK, B_KEY_DIM)
        v3 = bi.reshape(nb, GLR_BLOCK, B_VAL_DIM)
        b_end = b3[:, GLR_BLOCK - 1:GLR_BLOCK, :]
        q_dec = (q3 * jnp.exp(b3)).astype(BF16)
        k_dec = (k3 * jnp.exp(b_end - b3)).reshape(tt, B_KEY_DIM).astype(BF16)

        o_tiles = []
        for r0 in range(0, GLR_BLOCK, SUBLANES):
            rows = slice(r0, r0 + SUBLANES)
            b_t, q_t = b3[:, rows, :], q3[:, rows, :]
            t_pos = r0 + t_idx[:, :SUBLANES, :]
            o_t = jnp.zeros((nb, SUBLANES, B_VAL_DIM), F32)
            for s in range(min(GLR_BLOCK, r0 + SUBLANES)):
                gap = b_t - b3[:, s:s + 1, :]
                dec = jnp.exp(gap if s <= r0 else jnp.where(t_pos >= s, gap, NEG_INF))
                a = jnp.sum(q_t * k3[:, s:s + 1, :] * dec, axis=-1, keepdims=True)
                o_t = o_t + a * v3[:, s:s + 1, :]
            o_tiles.append(o_t)
        o_in = jnp.concatenate(o_tiles, axis=1)

        v_t = bi.T
        lhs = jnp.where(in_block, v_t[None], 0.0).astype(BF16).reshape(nb * B_VAL_DIM, tt)
        u_t = _dot(lhs, k_dec).reshape(nb, B_VAL_DIM, B_KEY_DIM)
        decay_end = jnp.exp(b_end)
        st = st_ref[hh]
        outs = []
        for jb in range(nb):
            outs.append(_dot_nt(q_dec[jb], st.astype(BF16)))
            st = st * decay_end[jb] + u_t[jb]
        st_ref[hh] = st
        o = jnp.concatenate(outs, axis=0) + o_in.reshape(tt, B_VAL_DIM)
        o = _rms(o, gw) * (bo * jax.nn.sigmoid(bo))
        o_ref[:, B_VAL_DIM * hh:B_VAL_DIM * (hh + 1)] = o.astype(o_ref.dtype)

    @pl.when(it == pl.num_programs(1) - 1)
    def _():
        for hh in range(B_HEADS):
            s_ref[hh] = st_ref[hh].T.astype(s_ref.dtype)


def _glr(bg4, lb_logits, glr_norm_w, s0, t_valid, tt, layer):
    b, t, _ = bg4.shape
    state = pl.BlockSpec((None, B_HEADS, B_KEY_DIM, B_VAL_DIM), lambda bi, i: (bi, 0, 0, 0))
    return pl.pallas_call(
        functools.partial(_glr_kernel, t_valid, layer),
        grid=(b, t // tt),
        in_specs=[pl.BlockSpec((None, tt, IN_GROUP_B), lambda bi, i: (bi, i, 0)),
                  pl.BlockSpec(lb_logits.shape, lambda bi, i: (0, 0)),
                  pl.BlockSpec((1, B_VAL_DIM), lambda bi, i: (0, 0)),
                  state],
        out_specs=(pl.BlockSpec((None, tt, B_V), lambda bi, i: (bi, i, 0)), state),
        out_shape=(jax.ShapeDtypeStruct((b, t, B_V), BF16), jax.ShapeDtypeStruct(s0.shape, s0.dtype)),
        scratch_shapes=[pltpu.VMEM((B_HEADS, B_VAL_DIM, B_KEY_DIM), F32)],
        compiler_params=_params("parallel", "arbitrary"),
    )(bg4, lb_logits, glr_norm_w, s0)


def _pool_kernel(n_hist, x_ref, prev_ref, g_ref, wp_ref, sc_ref, o_ref, hist_ref, carry_ref):
    tt, d = x_ref.shape
    group = d // len(POOL_WINDOWS)
    it = pl.program_id(1)

    @pl.when(it == 0)
    def _():
        carry_ref[...] = prev_ref[...]

    x = x_ref[...]
    h = _rms(x, g_ref[...])
    ext = jnp.concatenate([carry_ref[...], h], axis=0)
    pos = it * tt + lax.broadcasted_iota(jnp.int32, (tt, 1), 0) + n_hist
    run = ext
    width = 1
    for gi, win in enumerate(POOL_WINDOWS):
        while width < win:
            run = run + pltpu.roll(run, width, 0)
            width *= 2
        sl = slice(group * gi, group * (gi + 1))
        cnt = jnp.minimum(pos + 1, win).astype(F32)
        z = run[POOL_HALO:, sl] / cnt - h[:, sl]
        y = _dot(z.astype(BF16), wp_ref[gi]) * sc_ref[:, sl]
        o_ref[:, sl] = x[:, sl] + y
    tail = ext[tt:, :]
    carry_ref[...] = tail
    hist_ref[...] = tail


def _pool(x, prev, g, w_pool, scale, n_hist, tt):
    b, t, d = x.shape
    hist_spec = pl.BlockSpec((None, POOL_HALO, d), lambda bi, i: (bi, 0, 0))
    return pl.pallas_call(
        functools.partial(_pool_kernel, n_hist),
        grid=(b, t // tt),
        in_specs=[pl.BlockSpec((None, tt, d), lambda bi, i: (bi, i, 0)), hist_spec,
                  pl.BlockSpec((1, d), lambda bi, i: (0, 0)),
                  pl.BlockSpec(w_pool.shape, lambda bi, i: (0, 0, 0)),
                  pl.BlockSpec((1, d), lambda bi, i: (0, 0))],
        out_specs=(pl.BlockSpec((None, tt, d), lambda bi, i: (bi, i, 0)), hist_spec),
        out_shape=(jax.ShapeDtypeStruct((b, t, d), F32), jax.ShapeDtypeStruct((b, POOL_HALO, d), F32)),
        scratch_shapes=[pltpu.VMEM((POOL_HALO, d), F32)],
        compiler_params=_params("parallel", "arbitrary"),
    )(x, prev, g, w_pool, scale)


def _ffn_ple_kernel(n_chunks, has_mix, final, *refs):
    if has_mix:
        att_ref, glr_ref, wa_ref, wb_ref = refs[:4]
        refs = refs[4:]
    x_ref, p_ref, gf_ref, wgu_ref, wd_ref, gp_ref, wpg_ref, wpe_ref, gfin_ref, o_ref = refs
    x = x_ref[...]
    if has_mix:
        x = x + _dot(att_ref[...], wa_ref[...]) + _dot(glr_ref[...], wb_ref[...])
    h = _rms(x, gf_ref[...]).astype(BF16)
    d_ff = wd_ref.shape[0]
    cw = d_ff // n_chunks
    acc = None
    for c in range(n_chunks):
        sl = slice(cw * c, cw * (c + 1))
        gate = _dot(h, wgu_ref[:, sl])
        up = _dot(h, wgu_ref[:, d_ff + cw * c:d_ff + cw * (c + 1)])
        part = _dot((gate * jax.nn.sigmoid(gate) * up).astype(BF16), wd_ref[sl, :])
        acc = part if acc is None else acc + part
    x = x + acc
    gate = jax.nn.sigmoid(_dot(_rms(x, gp_ref[...]).astype(BF16), wpg_ref[...]))
    x = x + _dot(p_ref[...].astype(BF16), wpe_ref[...]) * gate
    if final:
        x = _rms(x, gfin_ref[...])
    o_ref[...] = x


def _ffn_ple(x, mix, p, g_ffn, w_gate_up, w_down, g_ple, w_ple_gate, w_ple, g_final, final, tm):
    n, d = x.shape
    d_ff = w_down.shape[0]
    n_chunks = 2 if d_ff % (2 * LANES) == 0 else 1
    row = lambda width: pl.BlockSpec((tm, width), lambda i: (i, 0))
    args = [x, p, g_ffn, w_gate_up, w_down, g_ple, w_ple_gate, w_ple, g_final]
    specs = [row(d), row(p.shape[1])] + [_const_spec(a.shape) for a in args[2:]]
    if mix is not None:
        args = list(mix) + args
        specs = [row(mix[0].shape[1]), row(mix[1].shape[1]), _const_spec(mix[2].shape), _const_spec(mix[3].shape)] + specs
    return pl.pallas_call(
        functools.partial(_ffn_ple_kernel, n_chunks, mix is not None, final),
        grid=(n // tm,),
        in_specs=specs,
        out_specs=row(d),
        out_shape=jax.ShapeDtypeStruct((n, d), F32),
        compiler_params=_params("parallel"),
    )(*args)


def _rope_tables(pos):
    half = A_HEAD_DIM // 2
    inv = ROPE_THETA ** (-jnp.arange(half, dtype=F32) / half)
    ang = pos.astype(F32)[:, None] * inv[None, :]
    c, s = jnp.cos(ang), jnp.sin(ang)
    reps = LANES // A_HEAD_DIM
    return jnp.tile(jnp.concatenate([c, c], axis=1), (1, reps)), jnp.tile(jnp.concatenate([-s, s], axis=1), (1, reps))


def _pack_w_in(w):
    d = w.shape[0]
    n_a = 3 * A_Q + IDX_Q
    n_i = IDX_DIM + IDX_HEADS
    return jnp.concatenate([w[:, :n_a], w[:, n_a:n_a + n_i], jnp.zeros((d, IN_GROUP_I - n_i), w.dtype),
                            w[:, n_a + n_i:]], axis=1).astype(BF16)


FFN_ROWS = 512


def _row_tile(n, want):
    t = min(n, want)
    assert n % t == 0
    return t


def _trunk(x, p, pos, attend, glr_s0, glr_tile, pool_prev, pool_hist, W):
    b, t, d = x.shape
    n = b * t
    tm = _row_tile(n, 256)
    row2 = lambda a: a.reshape(1, -1)
    cos_tab, sin_tab = _rope_tables(pos)
    depth = W["mix_norm"].shape[0]
    xf = x.reshape(n, d)
    outs = {}
    for i in range(depth):
        jj = i // 2
        if i % 2 == 0:
            bs, ts = (b, t) if t % tm == 0 else (1, n)
            res = _in_proj(xf.reshape(bs, ts, d), row2(W["mix_norm"][i]), _pack_w_in(W["w_in"][jj]),
                           cos_tab, sin_tab, tm)
            flat = lambda a: a.reshape(n, a.shape[2])
            qb, kt, kb, vt, vb, vtb, qib, kiw, kit, wt, ki2, bg4 = res
            qb, kb, vb, qib, kiw, ki2, bg4 = [flat(a) for a in (qb, kb, vb, qib, kiw, ki2, bg4)]
            att = attend(jj, qb, qib, kiw, ki2, kb, vb, wt, vtb)
            t_pad = -(-t // glr_tile) * glr_tile
            bg3 = bg4.reshape(b, t, IN_GROUP_B)
            if t_pad != t:
                bg3 = jnp.pad(bg3, ((0, 0), (0, t_pad - t), (0, 0)))
            glr, s_new = _glr(bg3, W["lb_logits"], row2(W["glr_norm"][jj]), glr_s0[jj], t, glr_tile, jj)
            glr = glr[:, :t].reshape(n, B_V)
            w_out = W["w_out"][jj].astype(BF16)
            mix = (att, glr, w_out[:A_Q], w_out[A_Q:])
            heads = lambda a: a.reshape(bs, A_HEADS, A_HEAD_DIM, ts).transpose(0, 3, 1, 2).reshape(
                b, t, A_HEADS, A_HEAD_DIM)
            outs.setdefault("k", []).append(heads(kt))
            outs.setdefault("v", []).append(heads(vt))
            outs.setdefault("ki", []).append(kit.transpose(0, 2, 1).reshape(b, t, IDX_DIM))
            outs.setdefault("s", []).append(s_new)
        else:
            tt = _row_tile(t, 256)
            t_pad = -(-t // 8) * 8
            x3 = xf.reshape(b, t, d)
            if t_pad != t:
                x3 = jnp.pad(x3, ((0, 0), (0, t_pad - t), (0, 0)))
                tt = t_pad
            y3, hist = _pool(x3, pool_prev[jj], row2(W["mix_norm"][i]), W["pool_w"][jj].astype(BF16),
                             row2(W["pool_scale"][jj]), pool_hist, tt)
            xf = y3[:, :t].reshape(n, d)
            mix = None
            outs.setdefault("hist", []).append((hist, t_pad))
        xf = _ffn_ple(xf, mix, p[i].reshape(n, -1), row2(W["ffn_norm"][i]), W["w_up"][i].astype(BF16),
                      W["w_down"][i].astype(BF16), row2(W["ple_norm"][i]), W["w_ple_gate"][i].astype(BF16),
                      W["w_ple"][i].astype(BF16), row2(W["final_norm"]), i == depth - 1, _row_tile(n, FFN_ROWS))
    return xf.reshape(b, t, d), outs


def kernel(x_prompt, x_sample, cache_k, cache_v, cache_kidx, state_glr, state_pool, page_table, p_prompt, p_sample,
           mix_norm, w_in, w_out, lb_logits, glr_norm, pool_w, pool_scale, ffn_norm, w_up, w_down, ple_norm,
           w_ple_gate, w_ple, final_norm):
    W = dict(mix_norm=mix_norm, w_in=w_in, w_out=w_out, lb_logits=lb_logits, glr_norm=glr_norm, pool_w=pool_w,
             pool_scale=pool_scale, ffn_norm=ffn_norm, w_up=w_up, w_down=w_down, ple_norm=ple_norm,
             w_ple_gate=w_ple_gate, w_ple=w_ple, final_norm=final_norm)
    n_ab = w_in.shape[0]
    n_c = pool_w.shape[0]
    d = x_prompt.shape[-1]
    n_buf = max(POOL_WINDOWS) - 1

    bp, sp = x_prompt.shape[:2]

    def attend_prompt(jj, qb, qib, kiw, ki2, kb, vb, wt, vtb):
        r3 = lambda a: a.reshape(bp, sp, a.shape[-1])
        return _dsa_prompt(r3(qb), r3(qib), wt, r3(ki2), r3(kb), vtb, vtb.shape[-1]).reshape(bp * sp, A_Q)

    glr0_p = [jnp.zeros((bp, B_HEADS, B_KEY_DIM, B_VAL_DIM), state_glr.dtype) for _ in range(n_ab)]
    pool0_p = [jnp.zeros((bp, POOL_HALO, d), F32) for _ in range(n_c)]
    yp, op = _trunk(x_prompt, p_prompt, jnp.arange(sp, dtype=jnp.int32), attend_prompt, glr0_p,
                    _row_tile(sp, 256), pool0_p, 0, W)

    bd, ts = x_sample.shape[:2]
    past = page_table.shape[1] * PAGE_SIZE

    n_pool = cache_k.shape[1]
    cache_kt = jnp.transpose(cache_k, (0, 1, 3, 4, 2)).reshape(n_ab, n_pool, A_Q, PAGE_SIZE)
    cache_vt = jnp.transpose(cache_v, (0, 1, 3, 4, 2)).reshape(n_ab, n_pool, A_Q, PAGE_SIZE)
    cache_kit = jnp.transpose(cache_kidx, (0, 1, 3, 2))

    def attend_sample(jj, qb, qib, kiw, ki2, kb, vb, wt, vtb):
        return _dsa_sample(qb, qib, kiw, ki2, kb, vb, cache_kt[jj], cache_vt[jj], cache_kit[jj], page_table, ts)

    pos_s = jnp.tile(past + jnp.arange(ts, dtype=jnp.int32), bd)
    pool0_s = [jnp.pad(state_pool[j].astype(F32), ((0, 0), (POOL_HALO - n_buf, 0), (0, 0))) for j in range(n_c)]
    ys, os_ = _trunk(x_sample, p_sample, pos_s, attend_sample, [state_glr[j] for j in range(n_ab)],
                     LANES, pool0_s, n_buf, W)

    def pool_rows(o, prev, t):
        res = []
        for j, (hist, t_pad) in enumerate(o["hist"]):
            if t >= n_buf:
                res.append(hist[:, POOL_HALO - (t_pad - t) - n_buf:POOL_HALO - (t_pad - t)])
            else:
                new = hist[:, POOL_HALO - t_pad:POOL_HALO - t_pad + t]
                res.append(jnp.concatenate([prev[j][:, t:].astype(new.dtype), new], axis=1))
        return jnp.stack(res)

    return (yp, ys, jnp.stack(op["k"]), jnp.stack(op["v"]), jnp.stack(op["ki"]), jnp.stack(op["s"]),
            pool_rows(op, None, sp).astype(x_prompt.dtype),
            jnp.stack(os_["k"]), jnp.stack(os_["v"]), jnp.stack(os_["ki"]), jnp.stack(os_["s"]),
            pool_rows(os_, state_pool, ts).astype(x_sample.dtype))
```
